```python
import jax, jax.numpy as jnp
from jax import lax
import numpy as np

D_MODEL = 2048
BATCH = 1
SEQ = 16384
DEPTH = 4

PLE_DIM = 256
D_FF = 4 * D_MODEL
MIX_WIDTH = D_MODEL
HG_WIDTH = MIX_WIDTH // 2
HG_HEAD_DIM = 128
HG_HEADS = HG_WIDTH // HG_HEAD_DIM
GLA_WIDTH = MIX_WIDTH - HG_WIDTH
GLA_HEADS = 4
GLA_DV = GLA_WIDTH // GLA_HEADS
GLA_DK = GLA_DV // 2
GLA_KEY_WIDTH = GLA_HEADS * GLA_DK
GLA_GATE_RANK = 16
GLA_GATE_NORM = 16.0
CONV_WIDTH = 4
CHUNK = 64
EPS = 1e-6

SPLIT_POINTS = (
    HG_WIDTH,
    2 * HG_WIDTH,
    3 * HG_WIDTH,
    4 * HG_WIDTH,
    4 * HG_WIDTH + GLA_KEY_WIDTH,
    4 * HG_WIDTH + 2 * GLA_KEY_WIDTH,
    4 * HG_WIDTH + 2 * GLA_KEY_WIDTH + GLA_WIDTH,
    4 * HG_WIDTH + 2 * GLA_KEY_WIDTH + 2 * GLA_WIDTH,
)
IN_WIDTH = 4 * HG_WIDTH + 2 * GLA_KEY_WIDTH + 2 * GLA_WIDTH + GLA_GATE_RANK
CONV_CH = 2 * GLA_KEY_WIDTH + GLA_WIDTH

kernel_name = "hgrn2_gla_parallel_hybrid"


def rmsnorm(x, g):
    xf = x.astype(jnp.float32)
    y = xf * lax.rsqrt(jnp.mean(xf * xf, axis=-1, keepdims=True) + EPS)
    return (y * g.astype(jnp.float32)).astype(x.dtype)


def head_rmsnorm(o, g):
    B, S, H, dv = o.shape
    y = o * lax.rsqrt(jnp.mean(o * o, axis=-1, keepdims=True) + EPS)
    return (y * g.astype(jnp.float32).reshape(H, dv)).reshape(B, S, H * dv)


def causal_short_conv(x, w):
    S = x.shape[1]
    xp = jnp.pad(x, ((0, 0), (CONV_WIDTH - 1, 0), (0, 0)))
    return sum(xp[:, j:j + S] * w[j] for j in range(CONV_WIDTH))


def chunked_gated_recurrence(q, k, v, log_a):
    B, S, H, dk = q.shape
    dv = v.shape[-1]
    n = S // CHUNK

    def to_chunks(t):
        return t.astype(jnp.float32).reshape(B, n, CHUNK, H, t.shape[-1]).transpose(1, 0, 3, 2, 4)

    qc, kc, vc, ac = to_chunks(q), to_chunks(k), to_chunks(v), to_chunks(log_a)
    causal = jnp.tril(jnp.ones((CHUNK, CHUNK), dtype=bool))[:, :, None]

    def step(state, inp):
        qi, ki, vi, ai = inp
        b = jnp.cumsum(ai, axis=2)
        diff = b[:, :, :, None, :] - b[:, :, None, :, :]
        decay = jnp.exp(jnp.where(causal, diff, -jnp.inf))
        scores = jnp.einsum('bhtc,bhsc,bhtsc->bhts', qi, ki, decay)
        o_intra = jnp.einsum('bhts,bhsv->bhtv', scores, vi)
        o_inter = jnp.einsum('bhtc,bhcv->bhtv', qi * jnp.exp(b), state)
        b_last = b[:, :, -1:, :]
        k_dec = ki * jnp.exp(b_last - b)
        new_state = jnp.exp(b_last[:, :, 0, :])[..., None] * state + jnp.einsum('bhsc,bhsv->bhcv', k_dec, vi)
        return new_state, o_intra + o_inter

    state0 = jnp.zeros((B, H, dk, dv), jnp.float32)
    _, o = lax.scan(step, state0, (qc, kc, vc, ac))
    return o.transpose(1, 0, 3, 2, 4).reshape(B, S, H, dv)


def setup_inputs(seed: int = 0) -> dict:
    key = jax.random.key(seed)
    ks = jax.random.split(key, 24)
    f32 = jnp.float32

    def nrm(k, shape, scale):
        return jax.random.normal(k, shape, f32) * scale

    def gain(k, shape):
        return 1.0 + 0.02 * jax.random.normal(k, shape, f32)

    return {
        "x": nrm(ks[0], (BATCH, SEQ, D_MODEL), 1.0),
        "p": nrm(ks[1], (DEPTH, BATCH, SEQ, PLE_DIM), 1.0),
        "g_mix": gain(ks[2], (DEPTH, D_MODEL)),
        "w_in": nrm(ks[3], (DEPTH, D_MODEL, IN_WIDTH), D_MODEL ** -0.5),
        "lb_logits": nrm(ks[4], (DEPTH, HG_WIDTH), 1.0),
        "g_hg_norm": gain(ks[5], (DEPTH, HG_WIDTH)),
        "conv_w": nrm(ks[6], (DEPTH, CONV_WIDTH, CONV_CH), CONV_WIDTH ** -0.5),
        "w_gla_gate": nrm(ks[7], (DEPTH, GLA_GATE_RANK, GLA_KEY_WIDTH), GLA_GATE_RANK ** -0.5),
        "b_gla_gate": nrm(ks[8], (DEPTH, GLA_KEY_WIDTH), 0.1),
        "g_gla_norm": gain(ks[9], (DEPTH, GLA_WIDTH)),
        "w_out": nrm(ks[10], (DEPTH, MIX_WIDTH, D_MODEL), MIX_WIDTH ** -0.5),
        "g_mlp": gain(ks[11], (DEPTH, D_MODEL)),
        "w_up": nrm(ks[12], (DEPTH, D_MODEL, D_FF), D_MODEL ** -0.5),
        "w_down": nrm(ks[13], (DEPTH, D_FF, D_MODEL), D_FF ** -0.5),
        "g_ple": gain(ks[14], (DEPTH, D_MODEL)),
        "w_pg": nrm(ks[15], (DEPTH, D_MODEL, D_MODEL), D_MODEL ** -0.5),
        "w_pp": nrm(ks[16], (DEPTH, PLE_DIM, D_MODEL), PLE_DIM ** -0.5),
        "g_final": gain(ks[17], (D_MODEL,)),
    }


def reference(x, p, g_mix, w_in, lb_logits, g_hg_norm, conv_w, w_gla_gate, b_gla_gate,
              g_gla_norm, w_out, g_mlp, w_up, w_down, g_ple, w_pg, w_pp, g_final):
    f32 = jnp.float32
    B, S, _ = x.shape
    lb_cum = jnp.cumsum(jax.nn.softmax(lb_logits.astype(f32), axis=0), axis=0)
    h = x
    for l in range(DEPTH):
        u = rmsnorm(h, g_mix[l])
        z = u @ w_in[l]
        hq, hf, hi, hg, gq, gk, gv, gr, ga = jnp.split(z, SPLIT_POINTS, axis=-1)

        lb = lb_cum[l] - lb_cum[0]
        log_f = jnp.logaddexp(jnp.log(lb), jnp.log1p(-lb) + jax.nn.log_sigmoid(hf.astype(f32)))
        k_h = -jnp.expm1(log_f)
        hs = (B, S, HG_HEADS, HG_HEAD_DIM)
        o_h = chunked_gated_recurrence(hq.reshape(hs), k_h.reshape(hs), hi.reshape(hs), log_f.reshape(hs))
        y_h = head_rmsnorm(o_h, g_hg_norm[l]) * jax.nn.silu(hg.astype(f32))

        qkv = jax.nn.silu(causal_short_conv(jnp.concatenate([gq, gk, gv], axis=-1), conv_w[l]))
        cq, ck, cv = jnp.split(qkv, (GLA_KEY_WIDTH, 2 * GLA_KEY_WIDTH), axis=-1)
        log_alpha = jax.nn.log_sigmoid((ga @ w_gla_gate[l] + b_gla_gate[l]).astype(f32)) / GLA_GATE_NORM
        ks_ = (B, S, GLA_HEADS, GLA_DK)
        o_g = chunked_gated_recurrence((cq * GLA_DK ** -0.5).reshape(ks_), ck.reshape(ks_),
                                       cv.reshape(B, S, GLA_HEADS, GLA_DV), log_alpha.reshape(ks_))
        y_g = head_rmsnorm(o_g, g_gla_norm[l]) * jax.nn.silu(gr.astype(f32))

        y = jnp.concatenate([y_h, y_g], axis=-1).astype(h.dtype)
        h = h + y @ w_out[l]

        m = rmsnorm(h, g_mlp[l]) @ w_up[l]
        h = h + jnp.square(jax.nn.relu(m)) @ w_down[l]

        gate = jax.nn.sigmoid(rmsnorm(h, g_ple[l]) @ w_pg[l])
        h = h + gate * (p[l] @ w_pp[l])
    return rmsnorm(h, g_final)
```

```python
import functools

import jax
import jax.numpy as jnp
from jax import lax
from jax.experimental import pallas as pl
from jax.experimental.pallas import tpu as pltpu

F32 = jnp.float32
BF16 = jnp.bfloat16

EPS = 1e-6
D_MODEL = 2048
D_FF = 4 * D_MODEL
PLE_DIM = 256
HG_WIDTH = 1024
HG_HEADS = 8
HEAD_DK = 128
HG_DV = 128
GLA_HEADS = 4
GLA_DV = 256
GLA_KEY_WIDTH = GLA_HEADS * HEAD_DK
GLA_WIDTH = GLA_HEADS * GLA_DV
GLA_GATE_RANK = 16
GLA_GATE_NORM = 16.0
CONV_WIDTH = 4
CONV_CH = 2 * GLA_KEY_WIDTH + GLA_WIDTH
IN_MAIN = 4 * HG_WIDTH + CONV_CH + GLA_WIDTH
LANES = 128
SUBLANES = 8

OFF_HQ, OFF_HF, OFF_HI, OFF_HG = 0, HG_WIDTH, 2 * HG_WIDTH, 3 * HG_WIDTH
OFF_CONV = 4 * HG_WIDTH
OFF_GR = OFF_CONV + CONV_CH
DECAY_W = HG_WIDTH + GLA_KEY_WIDTH

CHUNK = 64
SUB = 16
assert CHUNK == 4 * SUB
LOG2_SUB = SUB.bit_length() - 1
LOG2_DK = HEAD_DK.bit_length() - 1
MIX_TILE = 256
HALO = SUBLANES

VMEM_LIMIT = 56 * 1024 * 1024


def _cparams(sem):
    return pltpu.CompilerParams(dimension_semantics=sem, vmem_limit_bytes=VMEM_LIMIT)


def _rms_scale(x, g):
    ms = jnp.mean(x * x, axis=-1, keepdims=True)
    return x * lax.rsqrt(ms + EPS) * g


def _dot(a, b):
    return jnp.dot(a, b, preferred_element_type=F32)


def _dot_nt(a, b):
    return lax.dot_general(a, b, (((1,), (1,)), ((), ())), preferred_element_type=F32)


def _dot_tn(a, b):
    return lax.dot_general(a, b, (((0,), (0,)), ((), ())), preferred_element_type=F32)


def _log_sigmoid(x):
    return jnp.minimum(x, 0.0) - jnp.log1p(jnp.exp(-jnp.abs(x)))


def _inproj_kernel(h_ref, g_ref, w_ref, wga_ref, z_ref, ga_ref, u_ref):
    @pl.when(pl.program_id(1) == 0)
    def _():
        u_ref[...] = _rms_scale(h_ref[...], g_ref[...]).astype(BF16)
        ga_ref[...] = _dot(u_ref[...], wga_ref[...])

    z_ref[...] = _dot(u_ref[...], w_ref[...]).astype(z_ref.dtype)


def _inproj(h, g, w_main, w_ga, tm=1024, tn=1024):
    s = h.shape[0]
    return pl.pallas_call(
        _inproj_kernel,
        grid=(s // tm, IN_MAIN // tn),
        in_specs=[
            pl.BlockSpec((tm, D_MODEL), lambda i, j: (i, 0)),
            pl.BlockSpec((1, D_MODEL), lambda i, j: (0, 0)),
            pl.BlockSpec((D_MODEL, tn), lambda i, j: (0, j)),
            pl.BlockSpec((D_MODEL, LANES), lambda i, j: (0, 0)),
        ],
        out_specs=[
            pl.BlockSpec((tm, tn), lambda i, j: (i, j)),
            pl.BlockSpec((tm, LANES), lambda i, j: (i, 0)),
        ],
        out_shape=[
            jax.ShapeDtypeStruct((s, IN_MAIN), BF16),
            jax.ShapeDtypeStruct((s, LANES), F32),
        ],
        scratch_shapes=[pltpu.VMEM((tm, D_MODEL), BF16)],
        compiler_params=_cparams(("parallel", "arbitrary")),
        name="inproj",
    )(h, g, w_main, w_ga)


def _head_chunk(q, k, v, b_ref, bcol, k_ref, kcol, st_ref, hidx, consts):
    e_mat, lane_half, m_diag, m_lvl1, m_lvl2 = consts

    def brow(j):
        return b_ref[pl.ds(HALO + j, 1), bcol]

    bv = b_ref[pl.ds(HALO, CHUNK), bcol]
    b_prev = brow(-1)
    b_last = brow(CHUNK - 1)
    v_bf = v.astype(BF16)

    st = st_ref[hidx]
    qe = (q * jnp.exp(bv - b_prev)).astype(BF16)
    o = _dot_nt(qe, st.astype(BF16))
    kd = (k * jnp.exp(b_last - bv)).astype(BF16)
    st_ref[hidx] = st * jnp.exp(b_last - b_prev) + _dot_tn(v_bf, kd)

    half = CHUNK // 2
    ref1 = jnp.concatenate([jnp.broadcast_to(brow(SUB - 1), (half, HEAD_DK)),
                            jnp.broadcast_to(brow(half + SUB - 1), (half, HEAD_DK))], axis=0)
    d1 = bv - ref1
    s1 = _dot_nt((q * jnp.exp(jnp.minimum(d1, 0.0))).astype(BF16),
                 (k * jnp.exp(jnp.minimum(-d1, 0.0))).astype(BF16))
    d2 = bv - brow(half - 1)
    s2 = _dot_nt((q * jnp.exp(jnp.minimum(d2, 0.0))).astype(BF16),
                 (k * jnp.exp(jnp.minimum(-d2, 0.0))).astype(BF16))

    slabs = []
    for d in range(CHUNK // SUB):
        qb = q[d * SUB:(d + 1) * SUB]
        bb = bv[d * SUB:(d + 1) * SUB]
        for m in range(SUB // 2):
            xs = []
            for j in (d * SUB + 2 * m, d * SUB + 2 * m + 1):
                e = jnp.exp(jnp.minimum(bb - brow(j), 0.0))
                krow = k_ref[pl.ds(j, 1), kcol]
                xs.append((qb * e * krow).astype(BF16))
            slabs.append(jnp.concatenate(xs, axis=1))
    red = _dot(jnp.concatenate(slabs, axis=0), e_mat)
    blocks = []
    for d in range(CHUNK // SUB):
        blk = jnp.zeros((SUB, LANES), F32)
        for m in range(SUB // 2):
            idx = d * (SUB // 2) + m
            blk = jnp.where(lane_half == idx, red[idx * SUB:(idx + 1) * SUB], blk)
        blocks.append(blk)
    sd = jnp.concatenate(blocks, axis=0)[:, :CHUNK]

    a = jnp.where(m_diag, sd, jnp.where(m_lvl1, s1, jnp.where(m_lvl2, s2, 0.0)))
    return o + _dot(a.astype(BF16), v_bf)


def _head_norm_gate(o, gain, gate):
    ms = jnp.mean(o * o, axis=-1, keepdims=True)
    return o * lax.rsqrt(ms + EPS) * gain * (gate * jax.nn.sigmoid(gate))


def _mixer_kernel(z_ref, ga_ref, lb_ref, ghg_ref, convw_ref, wgate_ref, bgate_ref, ggla_ref,
                  y_ref, sth_ref, stg_ref, xpad_ref, kh_ref, qkv_ref, b_ref):
    tile = y_ref.shape[0]

    @pl.when(pl.program_id(0) == 0)
    def _():
        sth_ref[...] = jnp.zeros_like(sth_ref)
        stg_ref[...] = jnp.zeros_like(stg_ref)
        xpad_ref[pl.ds(0, HALO), :] = jnp.zeros((HALO, CONV_CH), F32)
        b_ref[pl.ds(0, HALO), :] = jnp.zeros((HALO, DECAY_W), F32)

    lb = lb_ref[...]
    hf = z_ref[:, OFF_HF:OFF_HF + HG_WIDTH].astype(F32)
    lhs = jnp.log(lb)
    log1m_lb = jnp.log1p(-lb)
    ls = _log_sigmoid(hf)
    rhs = log1m_lb + ls
    log_f = jnp.maximum(lhs, rhs) + jnp.log1p(jnp.exp(-jnp.abs(lhs - rhs)))
    kh_ref[...] = jnp.exp(log1m_lb + (ls - hf))

    xpad_ref[pl.ds(HALO, tile), :] = z_ref[:, OFF_CONV:OFF_CONV + CONV_CH].astype(F32)
    conv = convw_ref[0:1, :] * xpad_ref[pl.ds(HALO - CONV_WIDTH + 1, tile), :]
    for j in range(1, CONV_WIDTH):
        conv = conv + convw_ref[j:j + 1, :] * xpad_ref[pl.ds(HALO - CONV_WIDTH + 1 + j, tile), :]
    qkv_ref[...] = conv * jax.nn.sigmoid(conv)
    xpad_ref[pl.ds(0, HALO), :] = xpad_ref[pl.ds(tile, HALO), :]
    gate_pre = _dot(ga_ref[...].astype(BF16), wgate_ref[...]) + bgate_ref[...]
    log_alpha = _log_sigmoid(gate_pre) * (1.0 / GLA_GATE_NORM)

    row = lax.broadcasted_iota(jnp.int32, (tile, tile), 0)
    col = lax.broadcasted_iota(jnp.int32, (tile, tile), 1)
    tri = (row >= col).astype(BF16)
    logd = jnp.concatenate([log_f, log_alpha], axis=1)
    hi = logd.astype(BF16)
    rem = logd - hi.astype(F32)
    mid = rem.astype(BF16)
    lo = (rem - mid.astype(F32)).astype(BF16)
    b_ref[pl.ds(HALO, tile), :] = _dot(tri, hi) + _dot(tri, mid) + _dot(tri, lo)

    kk = lax.broadcasted_iota(jnp.int32, (2 * HEAD_DK, LANES), 0)
    nn = lax.broadcasted_iota(jnp.int32, (2 * HEAD_DK, LANES), 1)
    e_mat = (jnp.right_shift(kk, LOG2_DK) == (nn & 1)).astype(BF16)
    lane_half = jnp.right_shift(lax.broadcasted_iota(jnp.int32, (SUB, LANES), 1), 1)
    tt = lax.broadcasted_iota(jnp.int32, (CHUNK, CHUNK), 0)
    ss = lax.broadcasted_iota(jnp.int32, (CHUNK, CHUNK), 1)
    tb, sb = jnp.right_shift(tt, LOG2_SUB), jnp.right_shift(ss, LOG2_SUB)
    m_diag = (tb == sb) & (tt >= ss)
    m_lvl1 = (tb == sb + 1) & ((tb & 1) == 1)
    m_lvl2 = (tb >= 2) & (sb < 2)
    consts = (e_mat, lane_half, m_diag, m_lvl1, m_lvl2)

    def chunk_body(c, carry):
        r0 = pl.multiple_of(c * CHUNK, CHUNK)
        rows = pl.ds(r0, CHUNK)
        b_view = b_ref.at[pl.ds(r0, HALO + CHUNK)]
        kh_view = kh_ref.at[rows]
        qkv_view = qkv_ref.at[rows]
        for h in range(HG_HEADS):
            cs = slice(h * HEAD_DK, (h + 1) * HEAD_DK)
            q = z_ref[rows, OFF_HQ + h * HEAD_DK:OFF_HQ + (h + 1) * HEAD_DK].astype(F32)
            v = z_ref[rows, OFF_HI + h * HG_DV:OFF_HI + (h + 1) * HG_DV].astype(F32)
            o = _head_chunk(q, kh_view[:, cs], v, b_view, cs, kh_view, cs, sth_ref, h, consts)
            gate = z_ref[rows, OFF_HG + h * HG_DV:OFF_HG + (h + 1) * HG_DV].astype(F32)
            y_ref[rows, h * HG_DV:(h + 1) * HG_DV] = _head_norm_gate(
                o, ghg_ref[:, h * HG_DV:(h + 1) * HG_DV], gate).astype(y_ref.dtype)
        for h in range(GLA_HEADS):
            qs = slice(h * HEAD_DK, (h + 1) * HEAD_DK)
            ks = slice(GLA_KEY_WIDTH + h * HEAD_DK, GLA_KEY_WIDTH + (h + 1) * HEAD_DK)
            vs = slice(2 * GLA_KEY_WIDTH + h * GLA_DV, 2 * GLA_KEY_WIDTH + (h + 1) * GLA_DV)
            bs = slice(HG_WIDTH + h * HEAD_DK, HG_WIDTH + (h + 1) * HEAD_DK)
            q = qkv_view[:, qs] * (HEAD_DK ** -0.5)
            o = _head_chunk(q, qkv_view[:, ks], qkv_view[:, vs], b_view, bs,
                            qkv_view, ks, stg_ref, h, consts)
            gate = z_ref[rows, OFF_GR + h * GLA_DV:OFF_GR + (h + 1) * GLA_DV].astype(F32)
            y_ref[rows, HG_WIDTH + h * GLA_DV:HG_WIDTH + (h + 1) * GLA_DV] = _head_norm_gate(
                o, ggla_ref[:, h * GLA_DV:(h + 1) * GLA_DV], gate).astype(y_ref.dtype)
        return carry

    lax.fori_loop(0, tile // CHUNK, chunk_body, 0)


def _mixer(z, ga, lb, g_hg, conv_w, w_gate, b_gate, g_gla, tile=MIX_TILE):
    s = z.shape[0]
    const = lambda i: (0, 0)
    return pl.pallas_call(
        _mixer_kernel,
        grid=(s // tile,),
        in_specs=[
            pl.BlockSpec((tile, IN_MAIN), lambda i: (i, 0)),
            pl.BlockSpec((tile, LANES), lambda i: (i, 0)),
            pl.BlockSpec((1, HG_WIDTH), const),
            pl.BlockSpec((1, HG_WIDTH), const),
            pl.BlockSpec((CONV_WIDTH, CONV_CH), const),
            pl.BlockSpec((LANES, GLA_KEY_WIDTH), const),
            pl.BlockSpec((1, GLA_KEY_WIDTH), const),
            pl.BlockSpec((1, GLA_WIDTH), const),
        ],
        out_specs=pl.BlockSpec((tile, D_MODEL), lambda i: (i, 0)),
        out_shape=jax.ShapeDtypeStruct((s, D_MODEL), BF16),
        scratch_shapes=[
            pltpu.VMEM((HG_HEADS, HG_DV, HEAD_DK), F32),
            pltpu.VMEM((GLA_HEADS, GLA_DV, HEAD_DK), F32),
            pltpu.VMEM((HALO + tile, CONV_CH), F32),
            pltpu.VMEM((tile, HG_WIDTH), F32),
            pltpu.VMEM((tile, CONV_CH), F32),
            pltpu.VMEM((HALO + tile, DECAY_W), F32),
        ],
        compiler_params=_cparams(("arbitrary",)),
        name="mixer",
    )(z, ga, lb, g_hg, conv_w, w_gate, b_gate, g_gla)


def _outproj_kernel(y_ref, w_ref, h_ref, o_ref):
    o_ref[...] = h_ref[...] + _dot(y_ref[...], w_ref[...])


def _outproj(y, w, h, tm=1024, tn=1024):
    s = h.shape[0]
    return pl.pallas_call(
        _outproj_kernel,
        grid=(s // tm, D_MODEL // tn),
        in_specs=[
            pl.BlockSpec((tm, D_MODEL), lambda i, j: (i, 0)),
            pl.BlockSpec((D_MODEL, tn), lambda i, j: (0, j)),
            pl.BlockSpec((tm, tn), lambda i, j: (i, j)),
        ],
        out_specs=pl.BlockSpec((tm, tn), lambda i, j: (i, j)),
        out_shape=jax.ShapeDtypeStruct((s, D_MODEL), F32),
        compiler_params=_cparams(("parallel", "arbitrary")),
        name="outproj",
    )(y, w, h)


def _mlp_kernel(h_ref, g_ref, wup_ref, wdown_ref, o_ref, u_ref):
    @pl.when(pl.program_id(1) == 0)
    def _():
        x = h_ref[...]
        u_ref[...] = _rms_scale(x, g_ref[...]).astype(BF16)
        o_ref[...] = x

    m = jnp.maximum(_dot(u_ref[...], wup_ref[...]), 0.0)
    o_ref[...] += _dot((m * m).astype(BF16), wdown_ref[...])


def _mlp(h, g, w_up, w_down, tm=512, tf=1024):
    s = h.shape[0]
    return pl.pallas_call(
        _mlp_kernel,
        grid=(s // tm, D_FF // tf),
        in_specs=[
            pl.BlockSpec((tm, D_MODEL), lambda i, f: (i, 0)),
            pl.BlockSpec((1, D_MODEL), lambda i, f: (0, 0)),
            pl.BlockSpec((D_MODEL, tf), lambda i, f: (0, f)),
            pl.BlockSpec((tf, D_MODEL), lambda i, f: (f, 0)),
        ],
        out_specs=pl.BlockSpec((tm, D_MODEL), lambda i, f: (i, 0)),
        out_shape=jax.ShapeDtypeStruct((s, D_MODEL), F32),
        scratch_shapes=[pltpu.VMEM((tm, D_MODEL), BF16)],
        compiler_params=_cparams(("parallel", "arbitrary")),
        name="mlp",
    )(h, g, w_up, w_down)


def _ple_kernel(h_ref, hres_ref, g_ref, wpg_ref, p_ref, wpp_ref, o_ref, u_ref):
    @pl.when(pl.program_id(1) == 0)
    def _():
        u_ref[...] = _rms_scale(h_ref[...], g_ref[...]).astype(BF16)

    gate = jax.nn.sigmoid(_dot(u_ref[...], wpg_ref[...]))
    o_ref[...] = hres_ref[...] + gate * _dot(p_ref[...].astype(BF16), wpp_ref[...])


def _ple(h, g, w_pg, p, w_pp, tm=512, tn=1024):
    s = h.shape[0]
    return pl.pallas_call(
        _ple_kernel,
        grid=(s // tm, D_MODEL // tn),
        in_specs=[
            pl.BlockSpec((tm, D_MODEL), lambda i, j: (i, 0)),
            pl.BlockSpec((tm, tn), lambda i, j: (i, j)),
            pl.BlockSpec((1, D_MODEL), lambda i, j: (0, 0)),
            pl.BlockSpec((D_MODEL, tn), lambda i, j: (0, j)),
            pl.BlockSpec((tm, PLE_DIM), lambda i, j: (i, 0)),
            pl.BlockSpec((PLE_DIM, tn), lambda i, j: (0, j)),
        ],
        out_specs=pl.BlockSpec((tm, tn), lambda i, j: (i, j)),
        out_shape=jax.ShapeDtypeStruct((s, D_MODEL), F32),
        scratch_shapes=[pltpu.VMEM((tm, D_MODEL), BF16)],
        compiler_params=_cparams(("parallel", "arbitrary")),
        name="ple",
    )(h, h, g, w_pg, p, w_pp)


def _final_norm_kernel(h_ref, g_ref, o_ref):
    o_ref[...] = _rms_scale(h_ref[...], g_ref[...])


def _final_norm(h, g, tm=512):
    s = h.shape[0]
    return pl.pallas_call(
        _final_norm_kernel,
        grid=(s // tm,),
        in_specs=[
            pl.BlockSpec((tm, D_MODEL), lambda i: (i, 0)),
            pl.BlockSpec((1, D_MODEL), lambda i: (0, 0)),
        ],
        out_specs=pl.BlockSpec((tm, D_MODEL), lambda i: (i, 0)),
        out_shape=jax.ShapeDtypeStruct((s, D_MODEL), F32),
        compiler_params=_cparams(("parallel",)),
        name="final_norm",
    )(h, g)


def kernel(x, p, g_mix, w_in, lb_logits, g_hg_norm, conv_w, w_gla_gate, b_gla_gate, g_gla_norm,
           w_out, g_mlp, w_up, w_down, g_ple, w_pg, w_pp, g_final):
    batch, seq, _ = x.shape
    depth = w_in.shape[0]
    assert batch == 1 and seq % 1024 == 0
    lb_cum = jnp.cumsum(jax.nn.softmax(lb_logits.astype(F32), axis=0), axis=0)
    lb_all = lb_cum - lb_cum[0:1]
    row = lambda a: a.reshape(1, -1).astype(F32)

    h = x.reshape(seq, D_MODEL)
    for l in range(depth):
        w_main = w_in[l, :, :IN_MAIN].astype(BF16)
        w_ga = jnp.pad(w_in[l, :, IN_MAIN:], ((0, 0), (0, LANES - GLA_GATE_RANK))).astype(BF16)
        w_gate = jnp.pad(w_gla_gate[l], ((0, LANES - GLA_GATE_RANK), (0, 0))).astype(BF16)
        z, ga = _inproj(h, row(g_mix[l]), w_main, w_ga)
        y = _mixer(z, ga, row(lb_all[l]), row(g_hg_norm[l]), conv_w[l].astype(F32), w_gate,
                   row(b_gla_gate[l]), row(g_gla_norm[l]))
        h = _outproj(y, w_out[l].astype(BF16), h)
        h = _mlp(h, row(g_mlp[l]), w_up[l].astype(BF16), w_down[l].astype(BF16))
        h = _ple(h, row(g_ple[l]), w_pg[l].astype(BF16), p[l].reshape(seq, PLE_DIM),
                 w_pp[l].astype(BF16))
    return _final_norm(h, row(g_final)).reshape(batch, seq, D_MODEL)
```

```python
import jax
import jax.numpy as jnp
from jax import lax
from jax.experimental import pallas as pl
from jax.experimental.pallas import tpu as pltpu

F32 = jnp.float32
BF16 = jnp.bfloat16

EPS = 1e-6
LOG2E = 1.4426950408889634
D_MODEL = 2048
D_FF = 4 * D_MODEL
PLE_DIM = 256
HG_WIDTH = 1024
HG_HEADS = 8
HEAD_DK = 128
HG_DV = 128
GLA_HEADS = 4
GLA_DV = 256
GLA_KEY_WIDTH = GLA_HEADS * HEAD_DK
GLA_WIDTH = GLA_HEADS * GLA_DV
GLA_GATE_RANK = 16
GLA_GATE_NORM = 16.0
CONV_WIDTH = 4
CONV_CH = 2 * GLA_KEY_WIDTH + GLA_WIDTH
IN_MAIN = 4 * HG_WIDTH + CONV_CH + GLA_WIDTH
LANES = 128
SUBLANES = 8

OFF_HQ, OFF_HF, OFF_HI, OFF_HG = 0, HG_WIDTH, 2 * HG_WIDTH, 3 * HG_WIDTH
OFF_CONV = 4 * HG_WIDTH
OFF_GR = OFF_CONV + CONV_CH
DECAY_W = HG_WIDTH + GLA_KEY_WIDTH

CHUNK = 64
SUB = 8
LOG2_SUB = SUB.bit_length() - 1
LOG2_DK = HEAD_DK.bit_length() - 1
MIX_TILE = 256
HALO = SUBLANES

VMEM_LIMIT = 56 * 1024 * 1024


def _cparams(sem):
    return pltpu.CompilerParams(dimension_semantics=sem, vmem_limit_bytes=VMEM_LIMIT)


def _rms_scale(x, g):
    ms = jnp.mean(x * x, axis=-1, keepdims=True)
    return x * lax.rsqrt(ms + EPS) * g


def _dot(a, b):
    return jnp.dot(a, b, preferred_element_type=F32)


def _dot_nt(a, b):
    return lax.dot_general(a, b, (((1,), (1,)), ((), ())), preferred_element_type=F32)


def _dot_tn(a, b):
    return lax.dot_general(a, b, (((0,), (0,)), ((), ())), preferred_element_type=F32)


def _inproj_kernel(h_ref, g_ref, w_ref, wga_ref, z_ref, ga_ref, u_ref):
    @pl.when(pl.program_id(1) == 0)
    def _():
        u_ref[...] = _rms_scale(h_ref[...], g_ref[...]).astype(BF16)
        ga_ref[...] = _dot(u_ref[...], wga_ref[...])

    z_ref[...] = _dot(u_ref[...], w_ref[...]).astype(z_ref.dtype)


def _inproj(h, g, w_main, w_ga, tm=1024, tn=1024):
    s = h.shape[0]
    return pl.pallas_call(
        _inproj_kernel,
        grid=(s // tm, IN_MAIN // tn),
        in_specs=[
            pl.BlockSpec((tm, D_MODEL), lambda i, j: (i, 0)),
            pl.BlockSpec((1, D_MODEL), lambda i, j: (0, 0)),
            pl.BlockSpec((D_MODEL, tn), lambda i, j: (0, j)),
            pl.BlockSpec((D_MODEL, LANES), lambda i, j: (0, 0)),
        ],
        out_specs=[
            pl.BlockSpec((tm, tn), lambda i, j: (i, j)),
            pl.BlockSpec((tm, LANES), lambda i, j: (i, 0)),
        ],
        out_shape=[
            jax.ShapeDtypeStruct((s, IN_MAIN), BF16),
            jax.ShapeDtypeStruct((s, LANES), F32),
        ],
        scratch_shapes=[pltpu.VMEM((tm, D_MODEL), BF16)],
        compiler_params=_cparams(("parallel", "arbitrary")),
        name="inproj",
    )(h, g, w_main, w_ga)


def _log2_1p_exp2_neg_abs(x):
    return jnp.log(1.0 + jnp.exp2(-jnp.abs(x))) * LOG2E


def _log2_sigmoid(x2):
    return jnp.minimum(x2, 0.0) - _log2_1p_exp2_neg_abs(x2)


def _head_chunk(q, v_bf, b_ref, bcol, keys, st_ref, hidx, consts):
    e_mat, sel_masks, m_diag, m_levels = consts
    log_keys = keys[0] == "log"
    kcol = keys[-1]

    def brow(j):
        return b_ref[pl.ds(HALO + j, 1), bcol]

    bv = b_ref[pl.ds(HALO, CHUNK), bcol]
    b_prev = brow(-1)
    b_last = brow(CHUNK - 1)
    if log_keys:
        lkv = keys[2][:, kcol]
    else:
        kv = keys[1][:, kcol]

    def q_side(r):
        return (q * jnp.exp2(jnp.minimum(bv - r, 0.0))).astype(BF16)

    def k_side(r):
        d = jnp.minimum(r - bv, 0.0)
        if log_keys:
            return jnp.exp2(d + lkv).astype(BF16)
        return (kv * jnp.exp2(d)).astype(BF16)

    st = st_ref[hidx]
    o = _dot_nt(q_side(b_prev), st.astype(BF16))
    st_ref[hidx] = st * jnp.exp2(b_last - b_prev) + _dot_tn(v_bf, k_side(b_last))

    a = jnp.zeros((CHUNK, CHUNK), F32)
    for n, mask in m_levels:
        refs = [brow(p * 2 * n + n - 1) for p in range(CHUNK // (2 * n))]
        if len(refs) > 1:
            refs = [jnp.broadcast_to(r, (2 * n, HEAD_DK)) for r in refs]
            ref = jnp.concatenate(refs, axis=0)
        else:
            ref = refs[0]
        a = jnp.where(mask, _dot_nt(q_side(ref), k_side(ref)), a)

    slabs = []
    for d in range(CHUNK // SUB):
        qb = q[d * SUB:(d + 1) * SUB]
        bb = bv[d * SUB:(d + 1) * SUB]
        for m in range(SUB // 2):
            xs = []
            for j in (d * SUB + 2 * m, d * SUB + 2 * m + 1):
                if log_keys:
                    e = jnp.exp2(jnp.minimum(bb - keys[1][pl.ds(j, 1), kcol],
                                             keys[2][pl.ds(j, 1), kcol]))
                else:
                    e = jnp.exp2(jnp.minimum(bb - brow(j), 0.0)) * keys[1][pl.ds(j, 1), kcol]
                xs.append(qb * e)
            slabs.append(jnp.concatenate(xs, axis=1))
    red = _dot(jnp.concatenate(slabs, axis=0).astype(BF16), e_mat)
    blocks = []
    for d in range(CHUNK // SUB):
        blk = jnp.zeros((SUB, LANES), F32)
        for m in range(SUB // 2):
            idx = d * (SUB // 2) + m
            blk = jnp.where(sel_masks[idx], red[idx * SUB:(idx + 1) * SUB], blk)
        blocks.append(blk)
    sd = jnp.concatenate(blocks, axis=0)[:, :CHUNK]

    a = jnp.where(m_diag, sd, a)
    return o + _dot(a.astype(BF16), v_bf)


def _head_norm_gate(o, gain, gate):
    ms = jnp.mean(o * o, axis=-1, keepdims=True)
    return o * lax.rsqrt(ms + EPS) * gain * (gate * jax.nn.sigmoid(gate))


def _silu(x):
    return x * jax.nn.sigmoid(x)


def _mixer_kernel(z_ref, ga_ref, lb_ref, ghg_ref, convw_ref, wgate_ref, bgate_ref, ggla_ref,
                  y_ref, sth_ref, stg_ref, xh_ref, c_ref, lk_ref, gq_ref, gk_ref, gv_ref, b_ref):
    tile = y_ref.shape[0]

    @pl.when(pl.program_id(0) == 0)
    def _():
        sth_ref[...] = jnp.zeros_like(sth_ref)
        stg_ref[...] = jnp.zeros_like(stg_ref)
        xh_ref[pl.ds(0, HALO), :] = jnp.zeros((HALO, CONV_CH), F32)
        b_ref[pl.ds(0, HALO), :] = jnp.zeros((HALO, DECAY_W), F32)

    lb = lb_ref[...]
    l2_lb = jnp.log(lb) * LOG2E
    l2_1m = jnp.log1p(-lb) * LOG2E
    h2 = z_ref[:, OFF_HF:OFF_HF + HG_WIDTH].astype(F32) * LOG2E
    rhs = l2_1m + _log2_sigmoid(h2)
    log2_f = jnp.maximum(l2_lb, rhs) + _log2_1p_exp2_neg_abs(l2_lb - rhs)
    lk = rhs - h2
    lk_ref[...] = lk

    g2 = (_dot(ga_ref[...].astype(BF16), wgate_ref[...]) + bgate_ref[...]) * LOG2E
    log2_alpha = _log2_sigmoid(g2) * (1.0 / GLA_GATE_NORM)

    row = lax.broadcasted_iota(jnp.int32, (tile, tile), 0)
    col = lax.broadcasted_iota(jnp.int32, (tile, tile), 1)
    lag = row - col
    tri = (lag >= 0).astype(BF16)
    logd = jnp.concatenate([log2_f, log2_alpha], axis=1)
    hi = logd.astype(BF16)
    rem = logd - hi.astype(F32)
    mid = rem.astype(BF16)
    lo = (rem - mid.astype(F32)).astype(BF16)
    bcum = _dot(tri, hi) + _dot(tri, mid) + _dot(tri, lo)
    b_ref[pl.ds(HALO, tile), :] = bcum
    c_ref[...] = bcum[:, :HG_WIDTH] - lk

    x_bf = z_ref[:, OFF_CONV:OFF_CONV + CONV_CH]
    conv = convw_ref[CONV_WIDTH - 1:CONV_WIDTH, :] * x_bf.astype(F32)
    for j in range(CONV_WIDTH - 1):
        shift = (lag == CONV_WIDTH - 1 - j).astype(BF16)
        conv = conv + convw_ref[j:j + 1, :] * _dot(shift, x_bf)
    xh_ref[pl.ds(HALO, HALO), :] = x_bf[:HALO].astype(F32)
    head = convw_ref[0:1, :] * xh_ref[pl.ds(HALO - CONV_WIDTH + 1, HALO), :]
    for j in range(1, CONV_WIDTH):
        head = head + convw_ref[j:j + 1, :] * xh_ref[pl.ds(HALO - CONV_WIDTH + 1 + j, HALO), :]
    xh_ref[pl.ds(0, HALO), :] = x_bf[tile - HALO:].astype(F32)

    def put_qkv(rows, act):
        gq_ref[rows, :] = act[:, :GLA_KEY_WIDTH] * (HEAD_DK ** -0.5)
        gk_ref[rows, :] = act[:, GLA_KEY_WIDTH:2 * GLA_KEY_WIDTH]
        gv_ref[rows, :] = act[:, 2 * GLA_KEY_WIDTH:].astype(BF16)

    put_qkv(pl.ds(0, tile), _silu(conv))
    put_qkv(pl.ds(0, HALO), _silu(head))

    kk = lax.broadcasted_iota(jnp.int32, (2 * HEAD_DK, LANES), 0)
    nn = lax.broadcasted_iota(jnp.int32, (2 * HEAD_DK, LANES), 1)
    e_mat = (jnp.right_shift(kk, LOG2_DK) == (nn & 1)).astype(BF16)
    lane_half = jnp.right_shift(lax.broadcasted_iota(jnp.int32, (SUB, LANES), 1), 1)
    sel_masks = [lane_half == idx for idx in range(CHUNK // 2)]
    tt = lax.broadcasted_iota(jnp.int32, (CHUNK, CHUNK), 0)
    ss = lax.broadcasted_iota(jnp.int32, (CHUNK, CHUNK), 1)
    m_diag = (jnp.right_shift(tt, LOG2_SUB) == jnp.right_shift(ss, LOG2_SUB)) & (tt >= ss)
    m_levels = []
    n = SUB
    while n < CHUNK:
        lg = n.bit_length() - 1
        m_levels.append((n, (jnp.right_shift(tt, lg + 1) == jnp.right_shift(ss, lg + 1))
                         & ((jnp.right_shift(tt, lg) & 1) == 1) & ((jnp.right_shift(ss, lg) & 1) == 0)))
        n *= 2
    consts = (e_mat, sel_masks, m_diag, m_levels)

    def chunk_body(c, carry):
        r0 = pl.multiple_of(c * CHUNK, CHUNK)
        rows = pl.ds(r0, CHUNK)
        b_view = b_ref.at[pl.ds(r0, HALO + CHUNK)]
        c_view, lk_view, gk_view = c_ref.at[rows], lk_ref.at[rows], gk_ref.at[rows]
        for h in range(HG_HEADS):
            cs = slice(h * HEAD_DK, (h + 1) * HEAD_DK)
            q = z_ref[rows, OFF_HQ + h * HEAD_DK:OFF_HQ + (h + 1) * HEAD_DK].astype(F32)
            v_bf = z_ref[rows, OFF_HI + h * HG_DV:OFF_HI + (h + 1) * HG_DV]
            o = _head_chunk(q, v_bf, b_view, cs, ("log", c_view, lk_view, cs), sth_ref, h, consts)
            gate = z_ref[rows, OFF_HG + h * HG_DV:OFF_HG + (h + 1) * HG_DV].astype(F32)
            y_ref[rows, h * HG_DV:(h + 1) * HG_DV] = _head_norm_gate(
                o, ghg_ref[:, h * HG_DV:(h + 1) * HG_DV], gate).astype(y_ref.dtype)
        for h in range(GLA_HEADS):
            ks = slice(h * HEAD_DK, (h + 1) * HEAD_DK)
            vs = slice(h * GLA_DV, (h + 1) * GLA_DV)
            bs = slice(HG_WIDTH + h * HEAD_DK, HG_WIDTH + (h + 1) * HEAD_DK)
            o = _head_chunk(gq_ref[rows, ks], gv_ref[rows, vs], b_view, bs, ("lin", gk_view, ks),
                            stg_ref, h, consts)
            gate = z_ref[rows, OFF_GR + h * GLA_DV:OFF_GR + (h + 1) * GLA_DV].astype(F32)
            y_ref[rows, HG_WIDTH + h * GLA_DV:HG_WIDTH + (h + 1) * GLA_DV] = _head_norm_gate(
                o, ggla_ref[:, vs], gate).astype(y_ref.dtype)
        return carry

    lax.fori_loop(0, tile // CHUNK, chunk_body, 0)


def _mixer(z, ga, lb, g_hg, conv_w, w_gate, b_gate, g_gla, tile=MIX_TILE):
    s = z.shape[0]
    const = lambda i: (0, 0)
    return pl.pallas_call(
        _mixer_kernel,
        grid=(s // tile,),
        in_specs=[
            pl.BlockSpec((tile, IN_MAIN), lambda i: (i, 0)),
            pl.BlockSpec((tile, LANES), lambda i: (i, 0)),
            pl.BlockSpec((1, HG_WIDTH), const),
            pl.BlockSpec((1, HG_WIDTH), const),
            pl.BlockSpec((CONV_WIDTH, CONV_CH), const),
            pl.BlockSpec((LANES, GLA_KEY_WIDTH), const),
            pl.BlockSpec((1, GLA_KEY_WIDTH), const),
            pl.BlockSpec((1, GLA_WIDTH), const),
        ],
        out_specs=pl.BlockSpec((tile, D_MODEL), lambda i: (i, 0)),
        out_shape=jax.ShapeDtypeStruct((s, D_MODEL), BF16),
        scratch_shapes=[
            pltpu.VMEM((HG_HEADS, HG_DV, HEAD_DK), F32),
            pltpu.VMEM((GLA_HEADS, GLA_DV, HEAD_DK), F32),
            pltpu.VMEM((2 * HALO, CONV_CH), F32),
            pltpu.VMEM((tile, HG_WIDTH), F32),
            pltpu.VMEM((tile, HG_WIDTH), F32),
            pltpu.VMEM((tile, GLA_KEY_WIDTH), F32),
            pltpu.VMEM((tile, GLA_KEY_WIDTH), F32),
            pltpu.VMEM((tile, GLA_WIDTH), BF16),
            pltpu.VMEM((HALO + tile, DECAY_W), F32),
        ],
        compiler_params=_cparams(("arbitrary",)),
        name="mixer",
    )(z, ga, lb, g_hg, conv_w, w_gate, b_gate, g_gla)


def _outproj_kernel(y_ref, w_ref, h_ref, o_ref):
    o_ref[...] = h_ref[...] + _dot(y_ref[...], w_ref[...])


def _outproj(y, w, h, tm=1024, tn=1024):
    s = h.shape[0]
    return pl.pallas_call(
        _outproj_kernel,
        grid=(s // tm, D_MODEL // tn),
        in_specs=[
            pl.BlockSpec((tm, D_MODEL), lambda i, j: (i, 0)),
            pl.BlockSpec((D_MODEL, tn), lambda i, j: (0, j)),
            pl.BlockSpec((tm, tn), lambda i, j: (i, j)),
        ],
        out_specs=pl.BlockSpec((tm, tn), lambda i, j: (i, j)),
        out_shape=jax.ShapeDtypeStruct((s, D_MODEL), F32),
        compiler_params=_cparams(("parallel", "arbitrary")),
        name="outproj",
    )(y, w, h)


def _mlp_kernel(h_ref, g_ref, wup_ref, wdown_ref, o_ref, u_ref):
    @pl.when(pl.program_id(1) == 0)
    def _():
        x = h_ref[...]
        u_ref[...] = _rms_scale(x, g_ref[...]).astype(BF16)
        o_ref[...] = x

    m = jnp.maximum(_dot(u_ref[...], wup_ref[...]), 0.0)
    o_ref[...] += _dot((m * m).astype(BF16), wdown_ref[...])


def _mlp(h, g, w_up, w_down, tm=512, tf=1024):
    s = h.shape[0]
    return pl.pallas_call(
        _mlp_kernel,
        grid=(s // tm, D_FF // tf),
        in_specs=[
            pl.BlockSpec((tm, D_MODEL), lambda i, f: (i, 0)),
            pl.BlockSpec((1, D_MODEL), lambda i, f: (0, 0)),
            pl.BlockSpec((D_MODEL, tf), lambda i, f: (0, f)),
            pl.BlockSpec((tf, D_MODEL), lambda i, f: (f, 0)),
        ],
        out_specs=pl.BlockSpec((tm, D_MODEL), lambda i, f: (i, 0)),
        out_shape=jax.ShapeDtypeStruct((s, D_MODEL), F32),
        scratch_shapes=[pltpu.VMEM((tm, D_MODEL), BF16)],
        compiler_params=_cparams(("parallel", "arbitrary")),
        name="mlp",
    )(h, g, w_up, w_down)


def _ple_kernel(h_ref, hres_ref, g_ref, wpg_ref, p_ref, wpp_ref, o_ref, u_ref):
    @pl.when(pl.program_id(1) == 0)
    def _():
        u_ref[...] = _rms_scale(h_ref[...], g_ref[...]).astype(BF16)

    gate = jax.nn.sigmoid(_dot(u_ref[...], wpg_ref[...]))
    o_ref[...] = hres_ref[...] + gate * _dot(p_ref[...].astype(BF16), wpp_ref[...])


def _ple(h, g, w_pg, p, w_pp, tm=512, tn=1024):
    s = h.shape[0]
    return pl.pallas_call(
        _ple_kernel,
        grid=(s // tm, D_MODEL // tn),
        in_specs=[
            pl.BlockSpec((tm, D_MODEL), lambda i, j: (i, 0)),
            pl.BlockSpec((tm, tn), lambda i, j: (i, j)),
            pl.BlockSpec((1, D_MODEL), lambda i, j: (0, 0)),
            pl.BlockSpec((D_MODEL, tn), lambda i, j: (0, j)),
            pl.BlockSpec((tm, PLE_DIM), lambda i, j: (i, 0)),
            pl.BlockSpec((PLE_DIM, tn), lambda i, j: (0, j)),
        ],
        out_specs=pl.BlockSpec((tm, tn), lambda i, j: (i, j)),
        out_shape=jax.ShapeDtypeStruct((s, D_MODEL), F32),
        scratch_shapes=[pltpu.VMEM((tm, D_MODEL), BF16)],
        compiler_params=_cparams(("parallel", "arbitrary")),
        name="ple",
    )(h, h, g, w_pg, p, w_pp)


def _final_norm_kernel(h_ref, g_ref, o_ref):
    o_ref[...] = _rms_scale(h_ref[...], g_ref[...])


def _final_norm(h, g, tm=512):
    s = h.shape[0]
    return pl.pallas_call(
        _final_norm_kernel,
        grid=(s // tm,),
        in_specs=[
            pl.BlockSpec((tm, D_MODEL), lambda i: (i, 0)),
            pl.BlockSpec((1, D_MODEL), lambda i: (0, 0)),
        ],
        out_specs=pl.BlockSpec((tm, D_MODEL), lambda i: (i, 0)),
        out_shape=jax.ShapeDtypeStruct((s, D_MODEL), F32),
        compiler_params=_cparams(("parallel",)),
        name="final_norm",
    )(h, g)


def kernel(x, p, g_mix, w_in, lb_logits, g_hg_norm, conv_w, w_gla_gate, b_gla_gate, g_gla_norm,
           w_out, g_mlp, w_up, w_down, g_ple, w_pg, w_pp, g_final):
    batch, seq, _ = x.shape
    depth = w_in.shape[0]
    assert batch == 1 and seq % 1024 == 0
    lb_cum = jnp.cumsum(jax.nn.softmax(lb_logits.astype(F32), axis=0), axis=0)
    lb_all = lb_cum - lb_cum[0:1]
    row = lambda a: a.reshape(1, -1).astype(F32)

    h = x.reshape(seq, D_MODEL)
    for l in range(depth):
        w_main = w_in[l, :, :IN_MAIN].astype(BF16)
        w_ga = jnp.pad(w_in[l, :, IN_MAIN:], ((0, 0), (0, LANES - GLA_GATE_RANK))).astype(BF16)
        w_gate = jnp.pad(w_gla_gate[l], ((0, LANES - GLA_GATE_RANK), (0, 0))).astype(BF16)
        z, ga = _inproj(h, row(g_mix[l]), w_main, w_ga)
        y = _mixer(z, ga, row(lb_all[l]), row(g_hg_norm[l]), conv_w[l].astype(F32), w_gate,
                   row(b_gla_gate[l]), row(g_gla_norm[l]))
        h = _outproj(y, w_out[l].astype(BF16), h)
        h = _mlp(h, row(g_mlp[l]), w_up[l].astype(BF16), w_down[l].astype(BF16))
        h = _ple(h, row(g_ple[l]), w_pg[l].astype(BF16), p[l].reshape(seq, PLE_DIM),
                 w_pp[l].astype(BF16))
    return _final_norm(h, row(g_final)).reshape(batch, seq, D_MODEL)
```

```python
import jax
import jax.numpy as jnp
from jax import lax
from jax.experimental import pallas as pl
from jax.experimental.pallas import tpu as pltpu

F32 = jnp.float32
BF16 = jnp.bfloat16

EPS = 1e-6
LOG2E = 1.4426950408889634
D_MODEL = 2048
D_FF = 4 * D_MODEL
PLE_DIM = 256
HG_WIDTH = 1024
HG_HEADS = 8
HEAD_DK = 128
HG_DV = 128
GLA_HEADS = 4
GLA_DV = 256
GLA_KEY_WIDTH = GLA_HEADS * HEAD_DK
GLA_WIDTH = GLA_HEADS * GLA_DV
GLA_GATE_RANK = 16
GLA_GATE_NORM = 16.0
CONV_WIDTH = 4
CONV_CH = 2 * GLA_KEY_WIDTH + GLA_WIDTH
IN_MAIN = 4 * HG_WIDTH + CONV_CH + GLA_WIDTH
LANES = 128
SUBLANES = 8

OFF_HQ, OFF_HF, OFF_HI, OFF_HG = 0, HG_WIDTH, 2 * HG_WIDTH, 3 * HG_WIDTH
OFF_CONV = 4 * HG_WIDTH
OFF_GR = OFF_CONV + CONV_CH
DECAY_W = HG_WIDTH + GLA_KEY_WIDTH

CHUNK = 64
SUB = 8
LOG2_SUB = SUB.bit_length() - 1
LOG2_DK = HEAD_DK.bit_length() - 1
MIX_TILE = 256
HALO = SUBLANES

VMEM_LIMIT = 56 * 1024 * 1024


def _cparams(sem):
    return pltpu.CompilerParams(dimension_semantics=sem, vmem_limit_bytes=VMEM_LIMIT)


def _rms_scale(x, g):
    ms = jnp.mean(x * x, axis=-1, keepdims=True)
    return x * lax.rsqrt(ms + EPS) * g


def _dot(a, b):
    return jnp.dot(a, b, preferred_element_type=F32)


def _dot_nt(a, b):
    return lax.dot_general(a, b, (((1,), (1,)), ((), ())), preferred_element_type=F32)


def _dot_tn(a, b):
    return lax.dot_general(a, b, (((0,), (0,)), ((), ())), preferred_element_type=F32)


def _inproj_kernel(h_ref, g_ref, w_ref, wga_ref, z_ref, ga_ref, u_ref):
    @pl.when(pl.program_id(1) == 0)
    def _():
        u_ref[...] = _rms_scale(h_ref[...], g_ref[...]).astype(BF16)
        ga_ref[...] = _dot(u_ref[...], wga_ref[...])

    z_ref[...] = _dot(u_ref[...], w_ref[...]).astype(z_ref.dtype)


def _inproj(h, g, w_main, w_ga, tm=1024, tn=1024):
    s = h.shape[0]
    return pl.pallas_call(
        _inproj_kernel,
        grid=(s // tm, IN_MAIN // tn),
        in_specs=[
            pl.BlockSpec((tm, D_MODEL), lambda i, j: (i, 0)),
            pl.BlockSpec((1, D_MODEL), lambda i, j: (0, 0)),
            pl.BlockSpec((D_MODEL, tn), lambda i, j: (0, j)),
            pl.BlockSpec((D_MODEL, LANES), lambda i, j: (0, 0)),
        ],
        out_specs=[
            pl.BlockSpec((tm, tn), lambda i, j: (i, j)),
            pl.BlockSpec((tm, LANES), lambda i, j: (i, 0)),
        ],
        out_shape=[
            jax.ShapeDtypeStruct((s, IN_MAIN), BF16),
            jax.ShapeDtypeStruct((s, LANES), F32),
        ],
        scratch_shapes=[pltpu.VMEM((tm, D_MODEL), BF16)],
        compiler_params=_cparams(("parallel", "arbitrary")),
        name="inproj",
    )(h, g, w_main, w_ga)


def _log2_1p_exp2_neg_abs(x):
    return jnp.log(1.0 + jnp.exp2(-jnp.abs(x))) * LOG2E


def _log2_sigmoid(x2):
    return jnp.minimum(x2, 0.0) - _log2_1p_exp2_neg_abs(x2)


def _pad_rows(x, start):
    parts = []
    if start:
        parts.append(jnp.zeros((start, x.shape[1]), x.dtype))
    parts.append(x)
    if start + x.shape[0] < CHUNK:
        parts.append(jnp.zeros((CHUNK - start - x.shape[0], x.shape[1]), x.dtype))
    return jnp.concatenate(parts, axis=0) if len(parts) > 1 else x


def _keys_log(keys):
    return keys[0] == "log"


def _diag_terms(q, b_ref, bcol, keys):
    kcol = keys[-1]
    bv = b_ref[pl.ds(HALO, CHUNK), bcol]
    slabs = []
    for d in range(CHUNK // SUB):
        qb = q[d * SUB:(d + 1) * SUB]
        bb = bv[d * SUB:(d + 1) * SUB]
        xs = []
        for j in range(d * SUB, (d + 1) * SUB):
            if _keys_log(keys):
                e = jnp.exp2(jnp.minimum(bb - keys[1][pl.ds(j, 1), kcol],
                                         keys[2][pl.ds(j, 1), kcol]))
            else:
                e = (jnp.exp2(jnp.minimum(bb - b_ref[pl.ds(HALO + j, 1), bcol], 0.0))
                     * keys[1][pl.ds(j, 1), kcol])
            xs.append(qb * e)
        slabs.append(jnp.concatenate(xs, axis=1))
    return jnp.concatenate(slabs, axis=0).astype(BF16)


def _head_chunk(q, v_bf, b_ref, bcol, keys, st_ref, stb_ref, hidx):
    log_keys = _keys_log(keys)
    kcol = keys[-1]

    def brow(j):
        return b_ref[pl.ds(HALO + j, 1), bcol]

    bv = b_ref[pl.ds(HALO, CHUNK), bcol]
    b_prev = brow(-1)
    b_last = brow(CHUNK - 1)
    if log_keys:
        lkv = keys[2][:, kcol]
    else:
        kv = keys[1][:, kcol]

    def q_side(r, lo=0, hi=CHUNK):
        return q[lo:hi] * jnp.exp2(bv[lo:hi] - r)

    def k_side(r, lo=0, hi=CHUNK):
        if log_keys:
            return jnp.exp2(r - bv[lo:hi] + lkv[lo:hi])
        return kv[lo:hi] * jnp.exp2(r - bv[lo:hi])

    o = _dot_nt(q_side(b_prev).astype(BF16), stb_ref[hidx])
    st = st_ref[hidx] * jnp.exp2(b_last - b_prev) + _dot_tn(v_bf, k_side(b_last).astype(BF16))
    st_ref[hidx] = st
    stb_ref[hidx] = st.astype(BF16)

    q_slabs, k_slabs = [], []
    n = SUB
    while n < CHUNK:
        for p in range(CHUNK // (2 * n)):
            left, right = 2 * n * p, 2 * n * p + n
            ref = brow(right - 1)
            q_slabs.append(_pad_rows(q_side(ref, right, right + n), right))
            k_slabs.append(_pad_rows(k_side(ref, left, right), left))
        n *= 2
    a = _dot_nt(jnp.concatenate(q_slabs, axis=1).astype(BF16),
                jnp.concatenate(k_slabs, axis=1).astype(BF16))
    return o, a


def _head_norm_gate(o, gain, gate):
    ms = jnp.mean(o * o, axis=-1, keepdims=True)
    return o * lax.rsqrt(ms + EPS) * gain * (gate * jax.nn.sigmoid(gate))


def _silu(x):
    return x * jax.nn.sigmoid(x)


def _mixer_kernel(z_ref, ga_ref, lb_ref, ghg_ref, convw_ref, wgate_ref, bgate_ref, ggla_ref,
                  y_ref, sth_ref, stg_ref, sthb_ref, stgb_ref, xh_ref, c_ref, lk_ref, gq_ref, gk_ref,
                  gv_ref, b_ref, o_ref, xd_ref, sd_ref):
    tile = y_ref.shape[0]

    @pl.when(pl.program_id(0) == 0)
    def _():
        sth_ref[...] = jnp.zeros_like(sth_ref)
        stg_ref[...] = jnp.zeros_like(stg_ref)
        sthb_ref[...] = jnp.zeros_like(sthb_ref)
        stgb_ref[...] = jnp.zeros_like(stgb_ref)
        xh_ref[pl.ds(0, HALO), :] = jnp.zeros((HALO, CONV_CH), F32)
        b_ref[pl.ds(0, HALO), :] = jnp.zeros((HALO, DECAY_W), F32)

    lb = lb_ref[...]
    l2_lb = jnp.log(lb) * LOG2E
    l2_1m = jnp.log1p(-lb) * LOG2E
    h2 = z_ref[:, OFF_HF:OFF_HF + HG_WIDTH].astype(F32) * LOG2E
    rhs = l2_1m + _log2_sigmoid(h2)
    log2_f = jnp.maximum(l2_lb, rhs) + _log2_1p_exp2_neg_abs(l2_lb - rhs)
    lk = rhs - h2
    lk_ref[...] = lk

    g2 = (_dot(ga_ref[...].astype(BF16), wgate_ref[...]) + bgate_ref[...]) * LOG2E
    log2_alpha = _log2_sigmoid(g2) * (1.0 / GLA_GATE_NORM)

    row = lax.broadcasted_iota(jnp.int32, (tile, tile), 0)
    col = lax.broadcasted_iota(jnp.int32, (tile, tile), 1)
    lag = row - col
    tri = (lag >= 0).astype(BF16)
    logd = jnp.concatenate([log2_f, log2_alpha], axis=1)
    hi = logd.astype(BF16)
    rem = logd - hi.astype(F32)
    mid = rem.astype(BF16)
    lo = (rem - mid.astype(F32)).astype(BF16)
    bcum = _dot(tri, hi) + _dot(tri, mid) + _dot(tri, lo)
    b_ref[pl.ds(HALO, tile), :] = bcum
    c_ref[...] = bcum[:, :HG_WIDTH] - lk

    x_bf = z_ref[:, OFF_CONV:OFF_CONV + CONV_CH]
    conv = convw_ref[CONV_WIDTH - 1:CONV_WIDTH, :] * x_bf.astype(F32)
    for j in range(CONV_WIDTH - 1):
        shift = (lag == CONV_WIDTH - 1 - j).astype(BF16)
        conv = conv + convw_ref[j:j + 1, :] * _dot(shift, x_bf)
    xh_ref[pl.ds(HALO, HALO), :] = x_bf[:HALO].astype(F32)
    head = convw_ref[0:1, :] * xh_ref[pl.ds(HALO - CONV_WIDTH + 1, HALO), :]
    for j in range(1, CONV_WIDTH):
        head = head + convw_ref[j:j + 1, :] * xh_ref[pl.ds(HALO - CONV_WIDTH + 1 + j, HALO), :]
    xh_ref[pl.ds(0, HALO), :] = x_bf[tile - HALO:].astype(F32)

    def put_qkv(rows, act):
        gq_ref[rows, :] = act[:, :GLA_KEY_WIDTH] * (HEAD_DK ** -0.5)
        gk_ref[rows, :] = act[:, GLA_KEY_WIDTH:2 * GLA_KEY_WIDTH]
        gv_ref[rows, :] = act[:, 2 * GLA_KEY_WIDTH:].astype(BF16)

    put_qkv(pl.ds(0, tile), _silu(conv))
    put_qkv(pl.ds(0, HALO), _silu(head))

    kk = lax.broadcasted_iota(jnp.int32, (SUB * HEAD_DK, LANES), 0)
    nn = lax.broadcasted_iota(jnp.int32, (SUB * HEAD_DK, LANES), 1)
    e_mat = (jnp.right_shift(kk, LOG2_DK) == (nn & (SUB - 1))).astype(BF16)
    tt = lax.broadcasted_iota(jnp.int32, (CHUNK, CHUNK), 0)
    ss = lax.broadcasted_iota(jnp.int32, (CHUNK, CHUNK), 1)
    m_diag = (jnp.right_shift(tt, LOG2_SUB) == jnp.right_shift(ss, LOG2_SUB)) & (tt >= ss)

    n_heads = HG_HEADS + GLA_HEADS
    group = 4

    def chunk_heads(r0):
        rows = pl.ds(r0, CHUNK)
        b_view = b_ref.at[pl.ds(r0, HALO + CHUNK)]
        c_view, lk_view, gk_view = c_ref.at[rows], lk_ref.at[rows], gk_ref.at[rows]
        heads = []
        for h in range(HG_HEADS):
            cs = slice(h * HEAD_DK, (h + 1) * HEAD_DK)
            q = z_ref[rows, OFF_HQ + h * HEAD_DK:OFF_HQ + (h + 1) * HEAD_DK].astype(F32)
            v_bf = z_ref[rows, OFF_HI + h * HG_DV:OFF_HI + (h + 1) * HG_DV]
            heads.append((q, v_bf, cs, ("log", c_view, lk_view, cs), sth_ref, sthb_ref, h,
                          slice(h * HG_DV, (h + 1) * HG_DV)))
        for h in range(GLA_HEADS):
            ks = slice(h * HEAD_DK, (h + 1) * HEAD_DK)
            vs = slice(h * GLA_DV, (h + 1) * GLA_DV)
            bs = slice(HG_WIDTH + h * HEAD_DK, HG_WIDTH + (h + 1) * HEAD_DK)
            heads.append((gq_ref[rows, ks], gv_ref[rows, vs], bs, ("lin", gk_view, ks), stg_ref, stgb_ref,
                          h, slice(HG_WIDTH + h * GLA_DV, HG_WIDTH + (h + 1) * GLA_DV)))
        return rows, b_view, heads

    def diag_group(chunk, g, slot):
        _, b_view, heads = chunk
        for i in range(g * group, (g + 1) * group):
            q, _, bcol, keys = heads[i][:4]
            xd_ref[pl.ds(i * CHUNK, CHUNK), :] = _diag_terms(q, b_view, bcol, keys)
        grows = pl.ds(g * group * CHUNK, group * CHUNK)
        sd_ref[slot, grows, :] = _dot(xd_ref[grows, :], e_mat)

    def rest_group(chunk, g, slot, pending):
        rows, b_view, heads = chunk
        for i in range(g * group, (g + 1) * group):
            q, v_bf, bcol, keys, st_ref, stb_ref, hidx, ocol = heads[i]
            o, a = _head_chunk(q, v_bf, b_view, bcol, keys, st_ref, stb_ref, hidx)
            if pending is not None:
                finish(*pending)
            pending = (rows, slot, i, o, a, v_bf, ocol)
        return pending

    def finish(rows, slot, i, o, a, v_bf, ocol):
        sd = sd_ref[slot, pl.ds(i * CHUNK, CHUNK), :]
        a = jnp.where(m_diag, sd[:, :CHUNK], a)
        o_ref[rows, ocol] = o + _dot(a.astype(BF16), v_bf)

    n_chunks = tile // CHUNK
    first = chunk_heads(0)
    for g in range(n_heads // group):
        diag_group(first, g, 0)

    def chunk_body(c, carry):
        cur = chunk_heads(pl.multiple_of(c * CHUNK, CHUNK))
        nxt = chunk_heads(pl.multiple_of((c + 1) * CHUNK, CHUNK))
        slot = c & 1
        pending = None
        for g in range(n_heads // group):
            diag_group(nxt, g, 1 - slot)
            pending = rest_group(cur, g, slot, pending)
        finish(*pending)
        return carry

    lax.fori_loop(0, n_chunks - 1, chunk_body, 0)
    last = chunk_heads((n_chunks - 1) * CHUNK)
    pending = None
    for g in range(n_heads // group):
        pending = rest_group(last, g, (n_chunks - 1) & 1, pending)
    finish(*pending)

    for h in range(HG_HEADS):
        cs = slice(h * HG_DV, (h + 1) * HG_DV)
        gate = z_ref[:, OFF_HG + h * HG_DV:OFF_HG + (h + 1) * HG_DV].astype(F32)
        y_ref[:, cs] = _head_norm_gate(o_ref[:, cs], ghg_ref[:, cs], gate).astype(y_ref.dtype)
    for h in range(GLA_HEADS):
        vs = slice(h * GLA_DV, (h + 1) * GLA_DV)
        gate = z_ref[:, OFF_GR + h * GLA_DV:OFF_GR + (h + 1) * GLA_DV].astype(F32)
        y_ref[:, HG_WIDTH + h * GLA_DV:HG_WIDTH + (h + 1) * GLA_DV] = _head_norm_gate(
            o_ref[:, HG_WIDTH + h * GLA_DV:HG_WIDTH + (h + 1) * GLA_DV], ggla_ref[:, vs],
            gate).astype(y_ref.dtype)


def _mixer(z, ga, lb, g_hg, conv_w, w_gate, b_gate, g_gla, tile=MIX_TILE):
    s = z.shape[0]
    const = lambda i: (0, 0)
    return pl.pallas_call(
        _mixer_kernel,
        grid=(s // tile,),
        in_specs=[
            pl.BlockSpec((tile, IN_MAIN), lambda i: (i, 0)),
            pl.BlockSpec((tile, LANES), lambda i: (i, 0)),
            pl.BlockSpec((1, HG_WIDTH), const),
            pl.BlockSpec((1, HG_WIDTH), const),
            pl.BlockSpec((CONV_WIDTH, CONV_CH), const),
            pl.BlockSpec((LANES, GLA_KEY_WIDTH), const),
            pl.BlockSpec((1, GLA_KEY_WIDTH), const),
            pl.BlockSpec((1, GLA_WIDTH), const),
        ],
        out_specs=pl.BlockSpec((tile, D_MODEL), lambda i: (i, 0)),
        out_shape=jax.ShapeDtypeStruct((s, D_MODEL), BF16),
        scratch_shapes=[
            pltpu.VMEM((HG_HEADS, HG_DV, HEAD_DK), F32),
            pltpu.VMEM((GLA_HEADS, GLA_DV, HEAD_DK), F32),
            pltpu.VMEM((HG_HEADS, HG_DV, HEAD_DK), BF16),
            pltpu.VMEM((GLA_HEADS, GLA_DV, HEAD_DK), BF16),
            pltpu.VMEM((2 * HALO, CONV_CH), F32),
            pltpu.VMEM((tile, HG_WIDTH), F32),
            pltpu.VMEM((tile, HG_WIDTH), F32),
            pltpu.VMEM((tile, GLA_KEY_WIDTH), F32),
            pltpu.VMEM((tile, GLA_KEY_WIDTH), F32),
            pltpu.VMEM((tile, GLA_WIDTH), BF16),
            pltpu.VMEM((HALO + tile, DECAY_W), F32),
            pltpu.VMEM((tile, D_MODEL), F32),
            pltpu.VMEM(((HG_HEADS + GLA_HEADS) * CHUNK, SUB * HEAD_DK), BF16),
            pltpu.VMEM((2, (HG_HEADS + GLA_HEADS) * CHUNK, LANES), F32),
        ],
        compiler_params=_cparams(("arbitrary",)),
        name="mixer",
    )(z, ga, lb, g_hg, conv_w, w_gate, b_gate, g_gla)


def _outproj_kernel(y_ref, w_ref, h_ref, o_ref):
    o_ref[...] = h_ref[...] + _dot(y_ref[...], w_ref[...])


def _outproj(y, w, h, tm=1024, tn=1024):
    s = h.shape[0]
    return pl.pallas_call(
        _outproj_kernel,
        grid=(s // tm, D_MODEL // tn),
        in_specs=[
            pl.BlockSpec((tm, D_MODEL), lambda i, j: (i, 0)),
            pl.BlockSpec((D_MODEL, tn), lambda i, j: (0, j)),
            pl.BlockSpec((tm, tn), lambda i, j: (i, j)),
        ],
        out_specs=pl.BlockSpec((tm, tn), lambda i, j: (i, j)),
        out_shape=jax.ShapeDtypeStruct((s, D_MODEL), F32),
        compiler_params=_cparams(("parallel", "arbitrary")),
        name="outproj",
    )(y, w, h)


def _mlp_kernel(h_ref, g_ref, wup_ref, wdown_ref, o_ref, u_ref):
    @pl.when(pl.program_id(1) == 0)
    def _():
        x = h_ref[...]
        u_ref[...] = _rms_scale(x, g_ref[...]).astype(BF16)
        o_ref[...] = x

    m = jnp.maximum(_dot(u_ref[...], wup_ref[...]), 0.0)
    o_ref[...] += _dot((m * m).astype(BF16), wdown_ref[...])


def _mlp(h, g, w_up, w_down, tm=512, tf=1024):
    s = h.shape[0]
    return pl.pallas_call(
        _mlp_kernel,
        grid=(s // tm, D_FF // tf),
        in_specs=[
            pl.BlockSpec((tm, D_MODEL), lambda i, f: (i, 0)),
            pl.BlockSpec((1, D_MODEL), lambda i, f: (0, 0)),
            pl.BlockSpec((D_MODEL, tf), lambda i, f: (0, f)),
            pl.BlockSpec((tf, D_MODEL), lambda i, f: (f, 0)),
        ],
        out_specs=pl.BlockSpec((tm, D_MODEL), lambda i, f: (i, 0)),
        out_shape=jax.ShapeDtypeStruct((s, D_MODEL), F32),
        scratch_shapes=[pltpu.VMEM((tm, D_MODEL), BF16)],
        compiler_params=_cparams(("parallel", "arbitrary")),
        name="mlp",
    )(h, g, w_up, w_down)


def _ple_kernel(h_ref, hres_ref, g_ref, wpg_ref, p_ref, wpp_ref, o_ref, u_ref):
    @pl.when(pl.program_id(1) == 0)
    def _():
        u_ref[...] = _rms_scale(h_ref[...], g_ref[...]).astype(BF16)

    gate = jax.nn.sigmoid(_dot(u_ref[...], wpg_ref[...]))
    o_ref[...] = hres_ref[...] + gate * _dot(p_ref[...].astype(BF16), wpp_ref[...])


def _ple(h, g, w_pg, p, w_pp, tm=512, tn=1024):
    s = h.shape[0]
    return pl.pallas_call(
        _ple_kernel,
        grid=(s // tm, D_MODEL // tn),
        in_specs=[
            pl.BlockSpec((tm, D_MODEL), lambda i, j: (i, 0)),
            pl.BlockSpec((tm, tn), lambda i, j: (i, j)),
            pl.BlockSpec((1, D_MODEL), lambda i, j: (0, 0)),
            pl.BlockSpec((D_MODEL, tn), lambda i, j: (0, j)),
            pl.BlockSpec((tm, PLE_DIM), lambda i, j: (i, 0)),
            pl.BlockSpec((PLE_DIM, tn), lambda i, j: (0, j)),
        ],
        out_specs=pl.BlockSpec((tm, tn), lambda i, j: (i, j)),
        out_shape=jax.ShapeDtypeStruct((s, D_MODEL), F32),
        scratch_shapes=[pltpu.VMEM((tm, D_MODEL), BF16)],
        compiler_params=_cparams(("parallel", "arbitrary")),
        name="ple",
    )(h, h, g, w_pg, p, w_pp)


def _final_norm_kernel(h_ref, g_ref, o_ref):
    o_ref[...] = _rms_scale(h_ref[...], g_ref[...])


def _final_norm(h, g, tm=512):
    s = h.shape[0]
    return pl.pallas_call(
        _final_norm_kernel,
        grid=(s // tm,),
        in_specs=[
            pl.BlockSpec((tm, D_MODEL), lambda i: (i, 0)),
            pl.BlockSpec((1, D_MODEL), lambda i: (0, 0)),
        ],
        out_specs=pl.BlockSpec((tm, D_MODEL), lambda i: (i, 0)),
        out_shape=jax.ShapeDtypeStruct((s, D_MODEL), F32),
        compiler_params=_cparams(("parallel",)),
        name="final_norm",
    )(h, g)


def kernel(x, p, g_mix, w_in, lb_logits, g_hg_norm, conv_w, w_gla_gate, b_gla_gate, g_gla_norm,
           w_out, g_mlp, w_up, w_down, g_ple, w_pg, w_pp, g_final):
    batch, seq, _ = x.shape
    depth = w_in.shape[0]
    assert batch == 1 and seq % 1024 == 0
    lb_cum = jnp.cumsum(jax.nn.softmax(lb_logits.astype(F32), axis=0), axis=0)
    lb_all = lb_cum - lb_cum[0:1]
    row = lambda a: a.reshape(1, -1).astype(F32)

    h = x.reshape(seq, D_MODEL)
    for l in range(depth):
        w_main = w_in[l, :, :IN_MAIN].astype(BF16)
        w_ga = jnp.pad(w_in[l, :, IN_MAIN:], ((0, 0), (0, LANES - GLA_GATE_RANK))).astype(BF16)
        w_gate = jnp.pad(w_gla_gate[l], ((0, LANES - GLA_GATE_RANK), (0, 0))).astype(BF16)
        z, ga = _inproj(h, row(g_mix[l]), w_main, w_ga)
        y = _mixer(z, ga, row(lb_all[l]), row(g_hg_norm[l]), conv_w[l].astype(F32), w_gate,
                   row(b_gla_gate[l]), row(g_gla_norm[l]))
        h = _outproj(y, w_out[l].astype(BF16), h)
        h = _mlp(h, row(g_mlp[l]), w_up[l].astype(BF16), w_down[l].astype(BF16))
        h = _ple(h, row(g_ple[l]), w_pg[l].astype(BF16), p[l].reshape(seq, PLE_DIM),
                 w_pp[l].astype(BF16))
    return _final_norm(h, row(g_final)).reshape(batch, seq, D_MODEL)
```

```python
import functools

import jax
import jax.numpy as jnp
from jax import lax
from jax.experimental import pallas as pl
from jax.experimental.pallas import tpu as pltpu

F32 = jnp.float32
BF16 = jnp.bfloat16

EPS = 1e-6
LOG2E = 1.4426950408889634
D_MODEL = 2048
D_FF = 4 * D_MODEL
PLE_DIM = 256
HG_WIDTH = 1024
HG_HEADS = 8
HEAD_DK = 128
HG_DV = 128
GLA_HEADS = 4
GLA_DV = 256
GLA_KEY_WIDTH = GLA_HEADS * HEAD_DK
GLA_WIDTH = GLA_HEADS * GLA_DV
GLA_GATE_RANK = 16
GLA_GATE_NORM = 16.0
CONV_WIDTH = 4
CONV_CH = 2 * GLA_KEY_WIDTH + GLA_WIDTH
IN_MAIN = 4 * HG_WIDTH + CONV_CH + GLA_WIDTH
LANES = 128
SUBLANES = 8

OFF_HQ, OFF_HF, OFF_HI, OFF_HG = 0, HG_WIDTH, 2 * HG_WIDTH, 3 * HG_WIDTH
OFF_CONV = 4 * HG_WIDTH
OFF_GR = OFF_CONV + CONV_CH
DECAY_W = HG_WIDTH + GLA_KEY_WIDTH

CHUNK = 64
SUB = 8
LOG2_SUB = SUB.bit_length() - 1
LOG2_DK = HEAD_DK.bit_length() - 1
MIX_TILE = 256
HALO = SUBLANES

VMEM_LIMIT = 56 * 1024 * 1024


def _cparams(sem):
    return pltpu.CompilerParams(dimension_semantics=sem, vmem_limit_bytes=VMEM_LIMIT)


def _rms_scale(x, g):
    ms = jnp.mean(x * x, axis=-1, keepdims=True)
    return x * lax.rsqrt(ms + EPS) * g


def _dot(a, b):
    return jnp.dot(a, b, preferred_element_type=F32)


def _dot_nt(a, b):
    return lax.dot_general(a, b, (((1,), (1,)), ((), ())), preferred_element_type=F32)


def _dot_tn(a, b):
    return lax.dot_general(a, b, (((0,), (0,)), ((), ())), preferred_element_type=F32)


def _inproj_kernel(h_ref, g_ref, w_ref, wga_ref, z_ref, ga_ref, u_ref):
    @pl.when(pl.program_id(1) == 0)
    def _():
        u_ref[...] = _rms_scale(h_ref[...], g_ref[...]).astype(BF16)
        ga_ref[...] = _dot(u_ref[...], wga_ref[...])

    z_ref[...] = _dot(u_ref[...], w_ref[...]).astype(z_ref.dtype)


def _inproj(h, g, w_main, layer, w_ga, tm=1024, tn=1792):
    s = h.shape[0]
    return pl.pallas_call(
        _inproj_kernel,
        grid=(s // tm, IN_MAIN // tn),
        in_specs=[
            pl.BlockSpec((tm, D_MODEL), lambda i, j: (i, 0)),
            pl.BlockSpec((1, D_MODEL), lambda i, j: (0, 0)),
            pl.BlockSpec((None, D_MODEL, tn), lambda i, j: (layer, 0, j)),
            pl.BlockSpec((D_MODEL, LANES), lambda i, j: (0, 0)),
        ],
        out_specs=[
            pl.BlockSpec((tm, tn), lambda i, j: (i, j)),
            pl.BlockSpec((tm, LANES), lambda i, j: (i, 0)),
        ],
        out_shape=[
            jax.ShapeDtypeStruct((s, IN_MAIN), BF16),
            jax.ShapeDtypeStruct((s, LANES), F32),
        ],
        scratch_shapes=[pltpu.VMEM((tm, D_MODEL), BF16)],
        compiler_params=_cparams(("parallel", "arbitrary")),
        name="inproj",
    )(h, g, w_main, w_ga)


def _log2_1p_exp2_neg_abs(x):
    return jnp.log(1.0 + jnp.exp2(-jnp.abs(x))) * LOG2E


def _log2_sigmoid(x2):
    return jnp.minimum(x2, 0.0) - _log2_1p_exp2_neg_abs(x2)


def _pad_rows(x, start):
    parts = []
    if start:
        parts.append(jnp.zeros((start, x.shape[1]), x.dtype))
    parts.append(x)
    if start + x.shape[0] < CHUNK:
        parts.append(jnp.zeros((CHUNK - start - x.shape[0], x.shape[1]), x.dtype))
    return jnp.concatenate(parts, axis=0) if len(parts) > 1 else x


def _keys_log(keys):
    return keys[0] == "log"


def _diag_terms(q, b_ref, bcol, keys):
    kcol = keys[-1]
    bv = b_ref[pl.ds(HALO, CHUNK), bcol]
    slabs = []
    for d in range(CHUNK // SUB):
        qb = q[d * SUB:(d + 1) * SUB]
        bb = bv[d * SUB:(d + 1) * SUB]
        xs = []
        for j in range(d * SUB, (d + 1) * SUB):
            if _keys_log(keys):
                e = jnp.exp2(jnp.minimum(bb - keys[1][pl.ds(j, 1), kcol],
                                         keys[2][pl.ds(j, 1), kcol]))
            else:
                e = (jnp.exp2(jnp.minimum(bb - b_ref[pl.ds(HALO + j, 1), bcol], 0.0))
                     * keys[1][pl.ds(j, 1), kcol])
            xs.append(qb * e)
        slabs.append(jnp.concatenate(xs, axis=1))
    return jnp.concatenate(slabs, axis=0).astype(BF16)


def _head_chunk(q, v_bf, b_ref, bcol, keys, st_ref, stb_ref, hidx):
    log_keys = _keys_log(keys)
    kcol = keys[-1]

    def brow(j):
        return b_ref[pl.ds(HALO + j, 1), bcol]

    bv = b_ref[pl.ds(HALO, CHUNK), bcol]
    b_prev = brow(-1)
    b_last = brow(CHUNK - 1)
    if log_keys:
        lkv = keys[2][:, kcol]
    else:
        kv = keys[1][:, kcol]

    def q_side(r, lo=0, hi=CHUNK):
        return q[lo:hi] * jnp.exp2(bv[lo:hi] - r)

    def k_side(r, lo=0, hi=CHUNK):
        if log_keys:
            return jnp.exp2(r - bv[lo:hi] + lkv[lo:hi])
        return kv[lo:hi] * jnp.exp2(r - bv[lo:hi])

    o = _dot_nt(q_side(b_prev).astype(BF16), stb_ref[hidx])
    st = st_ref[hidx] * jnp.exp2(b_last - b_prev) + _dot_tn(v_bf, k_side(b_last).astype(BF16))
    st_ref[hidx] = st
    stb_ref[hidx] = st.astype(BF16)

    q_slabs, k_slabs = [], []
    n = SUB
    while n < CHUNK:
        for p in range(CHUNK // (2 * n)):
            left, right = 2 * n * p, 2 * n * p + n
            ref = brow(right - 1)
            q_slabs.append(_pad_rows(q_side(ref, right, right + n), right))
            k_slabs.append(_pad_rows(k_side(ref, left, right), left))
        n *= 2
    a = _dot_nt(jnp.concatenate(q_slabs, axis=1).astype(BF16),
                jnp.concatenate(k_slabs, axis=1).astype(BF16))
    return o, a


def _head_norm_gate(o, gain, gate):
    ms = jnp.mean(o * o, axis=-1, keepdims=True)
    return o * lax.rsqrt(ms + EPS) * gain * (gate * jax.nn.sigmoid(gate))


def _silu(x):
    return x * jax.nn.sigmoid(x)


def _mixer_kernel(z_ref, ga_ref, lb_ref, ghg_ref, convw_ref, wgate_ref, bgate_ref, ggla_ref,
                  y_ref, sth_ref, stg_ref, sthb_ref, stgb_ref, xh_ref, c_ref, lk_ref, gq_ref, gk_ref,
                  gv_ref, b_ref, o_ref, xd_ref, sd_ref):
    tile = y_ref.shape[0]

    @pl.when(pl.program_id(0) == 0)
    def _():
        sth_ref[...] = jnp.zeros_like(sth_ref)
        stg_ref[...] = jnp.zeros_like(stg_ref)
        sthb_ref[...] = jnp.zeros_like(sthb_ref)
        stgb_ref[...] = jnp.zeros_like(stgb_ref)
        xh_ref[pl.ds(0, HALO), :] = jnp.zeros((HALO, CONV_CH), F32)
        b_ref[pl.ds(0, HALO), :] = jnp.zeros((HALO, DECAY_W), F32)

    lb = lb_ref[...]
    l2_lb = jnp.log(lb) * LOG2E
    l2_1m = jnp.log1p(-lb) * LOG2E
    h2 = z_ref[:, OFF_HF:OFF_HF + HG_WIDTH].astype(F32) * LOG2E
    rhs = l2_1m + _log2_sigmoid(h2)
    log2_f = jnp.maximum(l2_lb, rhs) + _log2_1p_exp2_neg_abs(l2_lb - rhs)
    lk = rhs - h2
    lk_ref[...] = lk

    g2 = (_dot(ga_ref[...].astype(BF16), wgate_ref[...]) + bgate_ref[...]) * LOG2E
    log2_alpha = _log2_sigmoid(g2) * (1.0 / GLA_GATE_NORM)

    row = lax.broadcasted_iota(jnp.int32, (tile, tile), 0)
    col = lax.broadcasted_iota(jnp.int32, (tile, tile), 1)
    lag = row - col
    tri = (lag >= 0).astype(BF16)
    logd = jnp.concatenate([log2_f, log2_alpha], axis=1)
    hi = logd.astype(BF16)
    rem = logd - hi.astype(F32)
    mid = rem.astype(BF16)
    lo = (rem - mid.astype(F32)).astype(BF16)
    bcum = _dot(tri, hi) + _dot(tri, mid) + _dot(tri, lo)
    b_ref[pl.ds(HALO, tile), :] = bcum
    c_ref[...] = bcum[:, :HG_WIDTH] - lk

    x_bf = z_ref[:, OFF_CONV:OFF_CONV + CONV_CH]
    conv = convw_ref[CONV_WIDTH - 1:CONV_WIDTH, :] * x_bf.astype(F32)
    for j in range(CONV_WIDTH - 1):
        shift = (lag == CONV_WIDTH - 1 - j).astype(BF16)
        conv = conv + convw_ref[j:j + 1, :] * _dot(shift, x_bf)
    xh_ref[pl.ds(HALO, HALO), :] = x_bf[:HALO].astype(F32)
    head = convw_ref[0:1, :] * xh_ref[pl.ds(HALO - CONV_WIDTH + 1, HALO), :]
    for j in range(1, CONV_WIDTH):
        head = head + convw_ref[j:j + 1, :] * xh_ref[pl.ds(HALO - CONV_WIDTH + 1 + j, HALO), :]
    xh_ref[pl.ds(0, HALO), :] = x_bf[tile - HALO:].astype(F32)

    def put_qkv(rows, act):
        gq_ref[rows, :] = act[:, :GLA_KEY_WIDTH] * (HEAD_DK ** -0.5)
        gk_ref[rows, :] = act[:, GLA_KEY_WIDTH:2 * GLA_KEY_WIDTH]
        gv_ref[rows, :] = act[:, 2 * GLA_KEY_WIDTH:].astype(BF16)

    put_qkv(pl.ds(0, tile), _silu(conv))
    put_qkv(pl.ds(0, HALO), _silu(head))

    kk = lax.broadcasted_iota(jnp.int32, (SUB * HEAD_DK, LANES), 0)
    nn = lax.broadcasted_iota(jnp.int32, (SUB * HEAD_DK, LANES), 1)
    e_mat = (jnp.right_shift(kk, LOG2_DK) == (nn & (SUB - 1))).astype(BF16)
    tt = lax.broadcasted_iota(jnp.int32, (CHUNK, CHUNK), 0)
    ss = lax.broadcasted_iota(jnp.int32, (CHUNK, CHUNK), 1)
    m_diag = (jnp.right_shift(tt, LOG2_SUB) == jnp.right_shift(ss, LOG2_SUB)) & (tt >= ss)

    n_heads = HG_HEADS + GLA_HEADS
    group = 4

    def chunk_heads(r0):
        rows = pl.ds(r0, CHUNK)
        b_view = b_ref.at[pl.ds(r0, HALO + CHUNK)]
        c_view, lk_view, gk_view = c_ref.at[rows], lk_ref.at[rows], gk_ref.at[rows]
        heads = []
        for h in range(HG_HEADS):
            cs = slice(h * HEAD_DK, (h + 1) * HEAD_DK)
            q = z_ref[rows, OFF_HQ + h * HEAD_DK:OFF_HQ + (h + 1) * HEAD_DK].astype(F32)
            v_bf = z_ref[rows, OFF_HI + h * HG_DV:OFF_HI + (h + 1) * HG_DV]
            heads.append((q, v_bf, cs, ("log", c_view, lk_view, cs), sth_ref, sthb_ref, h,
                          slice(h * HG_DV, (h + 1) * HG_DV)))
        for h in range(GLA_HEADS):
            ks = slice(h * HEAD_DK, (h + 1) * HEAD_DK)
            vs = slice(h * GLA_DV, (h + 1) * GLA_DV)
            bs = slice(HG_WIDTH + h * HEAD_DK, HG_WIDTH + (h + 1) * HEAD_DK)
            heads.append((gq_ref[rows, ks], gv_ref[rows, vs], bs, ("lin", gk_view, ks), stg_ref, stgb_ref,
                          h, slice(HG_WIDTH + h * GLA_DV, HG_WIDTH + (h + 1) * GLA_DV)))
        return rows, b_view, heads

    def diag_group(chunk, g, slot):
        _, b_view, heads = chunk
        for i in range(g * group, (g + 1) * group):
            q, _, bcol, keys = heads[i][:4]
            xd_ref[pl.ds(i * CHUNK, CHUNK), :] = _diag_terms(q, b_view, bcol, keys)
        grows = pl.ds(g * group * CHUNK, group * CHUNK)
        sd_ref[slot, grows, :] = _dot(xd_ref[grows, :], e_mat)

    def rest_group(chunk, g, slot, pending):
        rows, b_view, heads = chunk
        for i in range(g * group, (g + 1) * group):
            q, v_bf, bcol, keys, st_ref, stb_ref, hidx, ocol = heads[i]
            o, a = _head_chunk(q, v_bf, b_view, bcol, keys, st_ref, stb_ref, hidx)
            if pending is not None:
                finish(*pending)
            pending = (rows, slot, i, o, a, v_bf, ocol)
        return pending

    def finish(rows, slot, i, o, a, v_bf, ocol):
        sd = sd_ref[slot, pl.ds(i * CHUNK, CHUNK), :]
        a = jnp.where(m_diag, sd[:, :CHUNK], a)
        o_ref[rows, ocol] = o + _dot(a.astype(BF16), v_bf)

    n_chunks = tile // CHUNK
    first = chunk_heads(0)
    for g in range(n_heads // group):
        diag_group(first, g, 0)

    def chunk_body(c, carry):
        cur = chunk_heads(pl.multiple_of(c * CHUNK, CHUNK))
        nxt = chunk_heads(pl.multiple_of((c + 1) * CHUNK, CHUNK))
        slot = c & 1
        pending = None
        for g in range(n_heads // group):
            diag_group(nxt, g, 1 - slot)
            pending = rest_group(cur, g, slot, pending)
        finish(*pending)
        return carry

    lax.fori_loop(0, n_chunks - 1, chunk_body, 0)
    last = chunk_heads((n_chunks - 1) * CHUNK)
    pending = None
    for g in range(n_heads // group):
        pending = rest_group(last, g, (n_chunks - 1) & 1, pending)
    finish(*pending)

    for h in range(HG_HEADS):
        cs = slice(h * HG_DV, (h + 1) * HG_DV)
        gate = z_ref[:, OFF_HG + h * HG_DV:OFF_HG + (h + 1) * HG_DV].astype(F32)
        y_ref[:, cs] = _head_norm_gate(o_ref[:, cs], ghg_ref[:, cs], gate).astype(y_ref.dtype)
    for h in range(GLA_HEADS):
        vs = slice(h * GLA_DV, (h + 1) * GLA_DV)
        gate = z_ref[:, OFF_GR + h * GLA_DV:OFF_GR + (h + 1) * GLA_DV].astype(F32)
        y_ref[:, HG_WIDTH + h * GLA_DV:HG_WIDTH + (h + 1) * GLA_DV] = _head_norm_gate(
            o_ref[:, HG_WIDTH + h * GLA_DV:HG_WIDTH + (h + 1) * GLA_DV], ggla_ref[:, vs],
            gate).astype(y_ref.dtype)


def _mixer(z, ga, lb, g_hg, conv_w, w_gate, b_gate, g_gla, tile=MIX_TILE):
    s = z.shape[0]
    const = lambda i: (0, 0)
    return pl.pallas_call(
        _mixer_kernel,
        grid=(s // tile,),
        in_specs=[
            pl.BlockSpec((tile, IN_MAIN), lambda i: (i, 0)),
            pl.BlockSpec((tile, LANES), lambda i: (i, 0)),
            pl.BlockSpec((1, HG_WIDTH), const),
            pl.BlockSpec((1, HG_WIDTH), const),
            pl.BlockSpec((CONV_WIDTH, CONV_CH), const),
            pl.BlockSpec((LANES, GLA_KEY_WIDTH), const),
            pl.BlockSpec((1, GLA_KEY_WIDTH), const),
            pl.BlockSpec((1, GLA_WIDTH), const),
        ],
        out_specs=pl.BlockSpec((tile, D_MODEL), lambda i: (i, 0)),
        out_shape=jax.ShapeDtypeStruct((s, D_MODEL), BF16),
        scratch_shapes=[
            pltpu.VMEM((HG_HEADS, HG_DV, HEAD_DK), F32),
            pltpu.VMEM((GLA_HEADS, GLA_DV, HEAD_DK), F32),
            pltpu.VMEM((HG_HEADS, HG_DV, HEAD_DK), BF16),
            pltpu.VMEM((GLA_HEADS, GLA_DV, HEAD_DK), BF16),
            pltpu.VMEM((2 * HALO, CONV_CH), F32),
            pltpu.VMEM((tile, HG_WIDTH), F32),
            pltpu.VMEM((tile, HG_WIDTH), F32),
            pltpu.VMEM((tile, GLA_KEY_WIDTH), F32),
            pltpu.VMEM((tile, GLA_KEY_WIDTH), F32),
            pltpu.VMEM((tile, GLA_WIDTH), BF16),
            pltpu.VMEM((HALO + tile, DECAY_W), F32),
            pltpu.VMEM((tile, D_MODEL), F32),
            pltpu.VMEM(((HG_HEADS + GLA_HEADS) * CHUNK, SUB * HEAD_DK), BF16),
            pltpu.VMEM((2, (HG_HEADS + GLA_HEADS) * CHUNK, LANES), F32),
        ],
        compiler_params=_cparams(("arbitrary",)),
        name="mixer",
    )(z, ga, lb, g_hg, conv_w, w_gate, b_gate, g_gla)


def _outproj_kernel(y_ref, w_ref, h_ref, o_ref):
    o_ref[...] = h_ref[...] + _dot(y_ref[...], w_ref[...])


def _outproj(y, w, layer, h, tm=1024, tn=1024):
    s = h.shape[0]
    return pl.pallas_call(
        _outproj_kernel,
        grid=(s // tm, D_MODEL // tn),
        in_specs=[
            pl.BlockSpec((tm, D_MODEL), lambda i, j: (i, 0)),
            pl.BlockSpec((None, D_MODEL, tn), lambda i, j: (layer, 0, j)),
            pl.BlockSpec((tm, tn), lambda i, j: (i, j)),
        ],
        out_specs=pl.BlockSpec((tm, tn), lambda i, j: (i, j)),
        out_shape=jax.ShapeDtypeStruct((s, D_MODEL), F32),
        compiler_params=_cparams(("parallel", "arbitrary")),
        name="outproj",
    )(y, w, h)


def _mlp_kernel(h_ref, g_ref, wup_ref, wdown_ref, o_ref, u_ref):
    @pl.when(pl.program_id(1) == 0)
    def _():
        x = h_ref[...]
        u_ref[...] = _rms_scale(x, g_ref[...]).astype(BF16)
        o_ref[...] = x

    m = jnp.maximum(_dot(u_ref[...], wup_ref[...]), 0.0)
    o_ref[...] += _dot((m * m).astype(BF16), wdown_ref[...])


def _mlp(h, g, w_up, w_down, layer, tm=512, tf=1024):
    s = h.shape[0]
    return pl.pallas_call(
        _mlp_kernel,
        grid=(s // tm, D_FF // tf),
        in_specs=[
            pl.BlockSpec((tm, D_MODEL), lambda i, f: (i, 0)),
            pl.BlockSpec((1, D_MODEL), lambda i, f: (0, 0)),
            pl.BlockSpec((None, D_MODEL, tf), lambda i, f: (layer, 0, f)),
            pl.BlockSpec((None, tf, D_MODEL), lambda i, f: (layer, f, 0)),
        ],
        out_specs=pl.BlockSpec((tm, D_MODEL), lambda i, f: (i, 0)),
        out_shape=jax.ShapeDtypeStruct((s, D_MODEL), F32),
        scratch_shapes=[pltpu.VMEM((tm, D_MODEL), BF16)],
        compiler_params=_cparams(("parallel", "arbitrary")),
        name="mlp",
    )(h, g, w_up, w_down)


def _ple_kernel(h_ref, g_ref, wpg_ref, p_ref, wpp_ref, *rest, final):
    x = h_ref[...]
    u = _rms_scale(x, g_ref[...]).astype(BF16)
    gate = jax.nn.sigmoid(_dot(u, wpg_ref[...]))
    y = x + gate * _dot(p_ref[...].astype(BF16), wpp_ref[...])
    if final:
        gf_ref, o_ref = rest
        y = _rms_scale(y, gf_ref[...])
    else:
        (o_ref,) = rest
    o_ref[...] = y


def _ple(h, g, w_pg, p, w_pp, layer, g_final=None, tm=512):
    s = h.shape[0]
    row = lambda i: (i, 0)
    const = lambda i: (0, 0)
    slab = lambda i: (layer, 0, 0)
    in_specs = [
        pl.BlockSpec((tm, D_MODEL), row),
        pl.BlockSpec((1, D_MODEL), const),
        pl.BlockSpec((None, D_MODEL, D_MODEL), slab),
        pl.BlockSpec((None, tm, PLE_DIM), lambda i: (layer, i, 0)),
        pl.BlockSpec((None, PLE_DIM, D_MODEL), slab),
    ]
    args = [h, g, w_pg, p, w_pp]
    if g_final is not None:
        in_specs.append(pl.BlockSpec((1, D_MODEL), const))
        args.append(g_final)
    return pl.pallas_call(
        functools.partial(_ple_kernel, final=g_final is not None),
        grid=(s // tm,),
        in_specs=in_specs,
        out_specs=pl.BlockSpec((tm, D_MODEL), row),
        out_shape=jax.ShapeDtypeStruct((s, D_MODEL), F32),
        compiler_params=_cparams(("parallel",)),
        name="ple",
    )(*args)


def kernel(x, p, g_mix, w_in, lb_logits, g_hg_norm, conv_w, w_gla_gate, b_gla_gate, g_gla_norm,
           w_out, g_mlp, w_up, w_down, g_ple, w_pg, w_pp, g_final):
    batch, seq, _ = x.shape
    depth = w_in.shape[0]
    assert batch == 1 and seq % 1024 == 0
    lb_cum = jnp.cumsum(jax.nn.softmax(lb_logits.astype(F32), axis=0), axis=0)
    lb_all = lb_cum - lb_cum[0:1]
    row = lambda a: a.reshape(1, -1).astype(F32)

    w_main = w_in[:, :, :IN_MAIN].astype(BF16)
    w_ga = jnp.pad(w_in[:, :, IN_MAIN:], ((0, 0), (0, 0), (0, LANES - GLA_GATE_RANK))).astype(BF16)
    w_gate = jnp.pad(w_gla_gate, ((0, 0), (0, LANES - GLA_GATE_RANK), (0, 0))).astype(BF16)
    w_out_b, w_up_b, w_down_b = w_out.astype(BF16), w_up.astype(BF16), w_down.astype(BF16)
    w_pg_b, w_pp_b = w_pg.astype(BF16), w_pp.astype(BF16)
    p2 = p.reshape(depth, seq, PLE_DIM)

    h = x.reshape(seq, D_MODEL)
    for l in range(depth):
        z, ga = _inproj(h, row(g_mix[l]), w_main, l, w_ga[l])
        y = _mixer(z, ga, row(lb_all[l]), row(g_hg_norm[l]), conv_w[l].astype(F32), w_gate[l],
                   row(b_gla_gate[l]), row(g_gla_norm[l]))
        h = _outproj(y, w_out_b, l, h)
        h = _mlp(h, row(g_mlp[l]), w_up_b, w_down_b, l)
        h = _ple(h, row(g_ple[l]), w_pg_b, p2, w_pp_b, l, row(g_final) if l == depth - 1 else None)
    return h.reshape(batch, seq, D_MODEL)
```

```python
import functools

import jax
import jax.numpy as jnp
from jax import lax
from jax.experimental import pallas as pl
from jax.experimental.pallas import tpu as pltpu

F32 = jnp.float32
BF16 = jnp.bfloat16

EPS = 1e-6
LOG2E = 1.4426950408889634
D_MODEL = 2048
D_FF = 4 * D_MODEL
PLE_DIM = 256
HG_WIDTH = 1024
HG_HEADS = 8
HEAD_DK = 128
HG_DV = 128
GLA_HEADS = 4
GLA_DV = 256
GLA_KEY_WIDTH = GLA_HEADS * HEAD_DK
GLA_WIDTH = GLA_HEADS * GLA_DV
GLA_GATE_RANK = 16
GLA_GATE_NORM = 16.0
CONV_WIDTH = 4
CONV_CH = 2 * GLA_KEY_WIDTH + GLA_WIDTH
IN_MAIN = 4 * HG_WIDTH + CONV_CH + GLA_WIDTH
LANES = 128
SUBLANES = 8

OFF_HQ, OFF_HF, OFF_HI, OFF_HG = 0, HG_WIDTH, 2 * HG_WIDTH, 3 * HG_WIDTH
OFF_CONV = 4 * HG_WIDTH
OFF_GR = OFF_CONV + CONV_CH
DECAY_W = HG_WIDTH + GLA_KEY_WIDTH

CHUNK = 64
SUB = 8
LOG2_SUB = SUB.bit_length() - 1
LOG2_DK = HEAD_DK.bit_length() - 1
MIX_TILE = 256
HALO = SUBLANES

VMEM_LIMIT = 56 * 1024 * 1024


def _cparams(sem):
    return pltpu.CompilerParams(dimension_semantics=sem, vmem_limit_bytes=VMEM_LIMIT)


def _rms_scale(x, g):
    ms = jnp.mean(x * x, axis=-1, keepdims=True)
    return x * lax.rsqrt(ms + EPS) * g


def _dot(a, b):
    return jnp.dot(a, b, preferred_element_type=F32)


def _dot_nt(a, b):
    return lax.dot_general(a, b, (((1,), (1,)), ((), ())), preferred_element_type=F32)


def _dot_tn(a, b):
    return lax.dot_general(a, b, (((0,), (0,)), ((), ())), preferred_element_type=F32)


def _inproj_kernel(h_ref, g_ref, w_ref, wga_ref, z_ref, ga_ref, u_ref):
    @pl.when(pl.program_id(1) == 0)
    def _():
        u_ref[...] = _rms_scale(h_ref[...], g_ref[...]).astype(BF16)
        ga_ref[...] = _dot(u_ref[...], wga_ref[...])

    z_ref[...] = _dot(u_ref[...], w_ref[...]).astype(z_ref.dtype)


def _inproj(h, g, w_in, layer, w_ga, tm=1024, tn=1792):
    s = h.shape[0]
    return pl.pallas_call(
        _inproj_kernel,
        grid=(s // tm, IN_MAIN // tn),
        in_specs=[
            pl.BlockSpec((tm, D_MODEL), lambda i, j: (i, 0)),
            pl.BlockSpec((1, D_MODEL), lambda i, j: (0, 0)),
            pl.BlockSpec((None, D_MODEL, tn), lambda i, j: (layer, 0, j)),
            pl.BlockSpec((D_MODEL, LANES), lambda i, j: (0, 0)),
        ],
        out_specs=[
            pl.BlockSpec((tm, tn), lambda i, j: (i, j)),
            pl.BlockSpec((tm, LANES), lambda i, j: (i, 0)),
        ],
        out_shape=[
            jax.ShapeDtypeStruct((s, IN_MAIN), BF16),
            jax.ShapeDtypeStruct((s, LANES), F32),
        ],
        scratch_shapes=[pltpu.VMEM((tm, D_MODEL), BF16)],
        compiler_params=_cparams(("parallel", "arbitrary")),
        name="inproj",
    )(h, g, w_in, w_ga)


def _log2_1p_exp2_neg_abs(x):
    return jnp.log(1.0 + jnp.exp2(-jnp.abs(x))) * LOG2E


def _log2_sigmoid(x2):
    return jnp.minimum(x2, 0.0) - _log2_1p_exp2_neg_abs(x2)


def _pad_rows(x, start):
    parts = []
    if start:
        parts.append(jnp.zeros((start, x.shape[1]), x.dtype))
    parts.append(x)
    if start + x.shape[0] < CHUNK:
        parts.append(jnp.zeros((CHUNK - start - x.shape[0], x.shape[1]), x.dtype))
    return jnp.concatenate(parts, axis=0) if len(parts) > 1 else x


def _keys_log(keys):
    return keys[0] == "log"


def _diag_terms(q, b_ref, bcol, keys):
    kcol = keys[-1]
    bv = b_ref[pl.ds(HALO, CHUNK), bcol]
    slabs = []
    for d in range(CHUNK // SUB):
        qb = q[d * SUB:(d + 1) * SUB]
        bb = bv[d * SUB:(d + 1) * SUB]
        xs = []
        for j in range(d * SUB, (d + 1) * SUB):
            if _keys_log(keys):
                e = jnp.exp2(jnp.minimum(bb - keys[1][pl.ds(j, 1), kcol],
                                         keys[2][pl.ds(j, 1), kcol]))
            else:
                e = (jnp.exp2(jnp.minimum(bb - b_ref[pl.ds(HALO + j, 1), bcol], 0.0))
                     * keys[1][pl.ds(j, 1), kcol])
            xs.append(qb * e)
        slabs.append(jnp.concatenate(xs, axis=1))
    return jnp.concatenate(slabs, axis=0).astype(BF16)


def _head_chunk(q, v_bf, b_ref, bcol, keys, st_ref, stb_ref, hidx):
    log_keys = _keys_log(keys)
    kcol = keys[-1]

    def brow(j):
        return b_ref[pl.ds(HALO + j, 1), bcol]

    bv = b_ref[pl.ds(HALO, CHUNK), bcol]
    b_prev = brow(-1)
    b_last = brow(CHUNK - 1)
    if log_keys:
        lkv = keys[2][:, kcol]
    else:
        kv = keys[1][:, kcol]

    def q_side(r, lo=0, hi=CHUNK):
        return q[lo:hi] * jnp.exp2(bv[lo:hi] - r)

    def k_side(r, lo=0, hi=CHUNK):
        if log_keys:
            return jnp.exp2(r - bv[lo:hi] + lkv[lo:hi])
        return kv[lo:hi] * jnp.exp2(r - bv[lo:hi])

    o = _dot_nt(q_side(b_prev).astype(BF16), stb_ref[hidx])
    st = st_ref[hidx] * jnp.exp2(b_last - b_prev) + _dot_tn(v_bf, k_side(b_last).astype(BF16))
    st_ref[hidx] = st
    stb_ref[hidx] = st.astype(BF16)

    q_slabs, k_slabs = [], []
    n = SUB
    while n < CHUNK:
        for p in range(CHUNK // (2 * n)):
            left, right = 2 * n * p, 2 * n * p + n
            ref = brow(right - 1)
            q_slabs.append(_pad_rows(q_side(ref, right, right + n), right))
            k_slabs.append(_pad_rows(k_side(ref, left, right), left))
        n *= 2
    a = _dot_nt(jnp.concatenate(q_slabs, axis=1).astype(BF16),
                jnp.concatenate(k_slabs, axis=1).astype(BF16))
    return o, a


def _head_norm_gate(o, gain, gate):
    ms = jnp.mean(o * o, axis=-1, keepdims=True)
    return o * lax.rsqrt(ms + EPS) * gain * (gate * jax.nn.sigmoid(gate))


def _silu(x):
    return x * jax.nn.sigmoid(x)


def _mixer_kernel(z_ref, ga_ref, lb_ref, ghg_ref, convw_ref, wgate_ref, bgate_ref, ggla_ref,
                  y_ref, sth_ref, stg_ref, sthb_ref, stgb_ref, xh_ref, c_ref, lk_ref, gq_ref, gk_ref,
                  gv_ref, b_ref, o_ref, xd_ref, sd_ref):
    tile = y_ref.shape[0]

    @pl.when(pl.program_id(0) == 0)
    def _():
        sth_ref[...] = jnp.zeros_like(sth_ref)
        stg_ref[...] = jnp.zeros_like(stg_ref)
        sthb_ref[...] = jnp.zeros_like(sthb_ref)
        stgb_ref[...] = jnp.zeros_like(stgb_ref)
        xh_ref[pl.ds(0, HALO), :] = jnp.zeros((HALO, CONV_CH), F32)
        b_ref[pl.ds(0, HALO), :] = jnp.zeros((HALO, DECAY_W), F32)

    lb = lb_ref[...]
    l2_lb = jnp.log(lb) * LOG2E
    l2_1m = jnp.log1p(-lb) * LOG2E
    h2 = z_ref[:, OFF_HF:OFF_HF + HG_WIDTH].astype(F32) * LOG2E
    rhs = l2_1m + _log2_sigmoid(h2)
    log2_f = jnp.maximum(l2_lb, rhs) + _log2_1p_exp2_neg_abs(l2_lb - rhs)
    lk = rhs - h2
    lk_ref[...] = lk

    g2 = (_dot(ga_ref[...].astype(BF16), wgate_ref[...]) + bgate_ref[...]) * LOG2E
    log2_alpha = _log2_sigmoid(g2) * (1.0 / GLA_GATE_NORM)

    row = lax.broadcasted_iota(jnp.int32, (tile, tile), 0)
    col = lax.broadcasted_iota(jnp.int32, (tile, tile), 1)
    lag = row - col
    tri = (lag >= 0).astype(BF16)
    logd = jnp.concatenate([log2_f, log2_alpha], axis=1)
    hi = logd.astype(BF16)
    lo = (logd - hi.astype(F32)).astype(BF16)
    bcum = _dot(tri, hi) + _dot(tri, lo)
    b_ref[pl.ds(HALO, tile), :] = bcum
    c_ref[...] = bcum[:, :HG_WIDTH] - lk

    x_bf = z_ref[:, OFF_CONV:OFF_CONV + CONV_CH]
    conv = convw_ref[CONV_WIDTH - 1:CONV_WIDTH, :] * x_bf.astype(F32)
    for j in range(CONV_WIDTH - 1):
        shift = (lag == CONV_WIDTH - 1 - j).astype(BF16)
        conv = conv + convw_ref[j:j + 1, :] * _dot(shift, x_bf)
    xh_ref[pl.ds(HALO, HALO), :] = x_bf[:HALO].astype(F32)
    head = convw_ref[0:1, :] * xh_ref[pl.ds(HALO - CONV_WIDTH + 1, HALO), :]
    for j in range(1, CONV_WIDTH):
        head = head + convw_ref[j:j + 1, :] * xh_ref[pl.ds(HALO - CONV_WIDTH + 1 + j, HALO), :]
    xh_ref[pl.ds(0, HALO), :] = x_bf[tile - HALO:].astype(F32)

    def put_qkv(rows, act):
        gq_ref[rows, :] = act[:, :GLA_KEY_WIDTH] * (HEAD_DK ** -0.5)
        gk_ref[rows, :] = act[:, GLA_KEY_WIDTH:2 * GLA_KEY_WIDTH]
        gv_ref[rows, :] = act[:, 2 * GLA_KEY_WIDTH:].astype(BF16)

    put_qkv(pl.ds(0, tile), _silu(conv))
    put_qkv(pl.ds(0, HALO), _silu(head))

    kk = lax.broadcasted_iota(jnp.int32, (SUB * HEAD_DK, LANES), 0)
    nn = lax.broadcasted_iota(jnp.int32, (SUB * HEAD_DK, LANES), 1)
    e_mat = (jnp.right_shift(kk, LOG2_DK) == (nn & (SUB - 1))).astype(BF16)
    tt = lax.broadcasted_iota(jnp.int32, (CHUNK, CHUNK), 0)
    ss = lax.broadcasted_iota(jnp.int32, (CHUNK, CHUNK), 1)
    m_diag = (jnp.right_shift(tt, LOG2_SUB) == jnp.right_shift(ss, LOG2_SUB)) & (tt >= ss)

    n_heads = HG_HEADS + GLA_HEADS
    group = 4
    lag = 3

    def chunk_heads(r0):
        rows = pl.ds(r0, CHUNK)
        b_view = b_ref.at[pl.ds(r0, HALO + CHUNK)]
        c_view, lk_view, gk_view = c_ref.at[rows], lk_ref.at[rows], gk_ref.at[rows]
        heads = []
        for h in range(HG_HEADS):
            cs = slice(h * HEAD_DK, (h + 1) * HEAD_DK)
            q = z_ref[rows, OFF_HQ + h * HEAD_DK:OFF_HQ + (h + 1) * HEAD_DK].astype(F32)
            v_bf = z_ref[rows, OFF_HI + h * HG_DV:OFF_HI + (h + 1) * HG_DV]
            heads.append((q, v_bf, cs, ("log", c_view, lk_view, cs), sth_ref, sthb_ref, h,
                          slice(h * HG_DV, (h + 1) * HG_DV)))
        for h in range(GLA_HEADS):
            ks = slice(h * HEAD_DK, (h + 1) * HEAD_DK)
            vs = slice(h * GLA_DV, (h + 1) * GLA_DV)
            bs = slice(HG_WIDTH + h * HEAD_DK, HG_WIDTH + (h + 1) * HEAD_DK)
            heads.append((gq_ref[rows, ks], gv_ref[rows, vs], bs, ("lin", gk_view, ks), stg_ref, stgb_ref,
                          h, slice(HG_WIDTH + h * GLA_DV, HG_WIDTH + (h + 1) * GLA_DV)))
        return rows, b_view, heads

    def diag_group(chunk, g, slot):
        _, b_view, heads = chunk
        for i in range(g * group, (g + 1) * group):
            q, _, bcol, keys = heads[i][:4]
            xd_ref[pl.ds(i * CHUNK, CHUNK), :] = _diag_terms(q, b_view, bcol, keys)
        grows = pl.ds(g * group * CHUNK, group * CHUNK)
        sd_ref[slot, grows, :] = _dot(xd_ref[grows, :], e_mat)

    def rest_group(chunk, g, slot, pending):
        rows, b_view, heads = chunk
        for i in range(g * group, (g + 1) * group):
            q, v_bf, bcol, keys, st_ref, stb_ref, hidx, ocol = heads[i]
            o, a = _head_chunk(q, v_bf, b_view, bcol, keys, st_ref, stb_ref, hidx)
            pending = pending + [(rows, slot, i, o, a, v_bf, ocol)]
            if len(pending) > lag:
                finish(*pending[0])
                pending = pending[1:]
        return pending

    def finish(rows, slot, i, o, a, v_bf, ocol):
        sd = sd_ref[slot, pl.ds(i * CHUNK, CHUNK), :]
        a = jnp.where(m_diag, sd[:, :CHUNK], a)
        o_ref[rows, ocol] = o + _dot(a.astype(BF16), v_bf)

    n_chunks = tile // CHUNK
    first = chunk_heads(0)
    for g in range(n_heads // group):
        diag_group(first, g, 0)

    def chunk_body(c, carry):
        cur = chunk_heads(pl.multiple_of(c * CHUNK, CHUNK))
        nxt = chunk_heads(pl.multiple_of((c + 1) * CHUNK, CHUNK))
        slot = c & 1
        pending = []
        for g in range(n_heads // group):
            diag_group(nxt, g, 1 - slot)
            pending = rest_group(cur, g, slot, pending)
        for item in pending:
            finish(*item)
        return carry

    lax.fori_loop(0, n_chunks - 1, chunk_body, 0)
    last = chunk_heads((n_chunks - 1) * CHUNK)
    pending = []
    for g in range(n_heads // group):
        pending = rest_group(last, g, (n_chunks - 1) & 1, pending)
    for item in pending:
        finish(*item)

    for h in range(HG_HEADS):
        cs = slice(h * HG_DV, (h + 1) * HG_DV)
        gate = z_ref[:, OFF_HG + h * HG_DV:OFF_HG + (h + 1) * HG_DV].astype(F32)
        y_ref[:, cs] = _head_norm_gate(o_ref[:, cs], ghg_ref[:, cs], gate).astype(y_ref.dtype)
    for h in range(GLA_HEADS):
        vs = slice(h * GLA_DV, (h + 1) * GLA_DV)
        gate = z_ref[:, OFF_GR + h * GLA_DV:OFF_GR + (h + 1) * GLA_DV].astype(F32)
        y_ref[:, HG_WIDTH + h * GLA_DV:HG_WIDTH + (h + 1) * GLA_DV] = _head_norm_gate(
            o_ref[:, HG_WIDTH + h * GLA_DV:HG_WIDTH + (h + 1) * GLA_DV], ggla_ref[:, vs],
            gate).astype(y_ref.dtype)


def _mixer(z, ga, lb, g_hg, conv_w, w_gate, b_gate, g_gla, tile=MIX_TILE):
    s = z.shape[0]
    const = lambda i: (0, 0)
    return pl.pallas_call(
        _mixer_kernel,
        grid=(s // tile,),
        in_specs=[
            pl.BlockSpec((tile, IN_MAIN), lambda i: (i, 0)),
            pl.BlockSpec((tile, LANES), lambda i: (i, 0)),
            pl.BlockSpec((1, HG_WIDTH), const),
            pl.BlockSpec((1, HG_WIDTH), const),
            pl.BlockSpec((CONV_WIDTH, CONV_CH), const),
            pl.BlockSpec((LANES, GLA_KEY_WIDTH), const),
            pl.BlockSpec((1, GLA_KEY_WIDTH), const),
            pl.BlockSpec((1, GLA_WIDTH), const),
        ],
        out_specs=pl.BlockSpec((tile, D_MODEL), lambda i: (i, 0)),
        out_shape=jax.ShapeDtypeStruct((s, D_MODEL), BF16),
        scratch_shapes=[
            pltpu.VMEM((HG_HEADS, HG_DV, HEAD_DK), F32),
            pltpu.VMEM((GLA_HEADS, GLA_DV, HEAD_DK), F32),
            pltpu.VMEM((HG_HEADS, HG_DV, HEAD_DK), BF16),
            pltpu.VMEM((GLA_HEADS, GLA_DV, HEAD_DK), BF16),
            pltpu.VMEM((2 * HALO, CONV_CH), F32),
            pltpu.VMEM((tile, HG_WIDTH), F32),
            pltpu.VMEM((tile, HG_WIDTH), F32),
            pltpu.VMEM((tile, GLA_KEY_WIDTH), F32),
            pltpu.VMEM((tile, GLA_KEY_WIDTH), F32),
            pltpu.VMEM((tile, GLA_WIDTH), BF16),
            pltpu.VMEM((HALO + tile, DECAY_W), F32),
            pltpu.VMEM((tile, D_MODEL), F32),
            pltpu.VMEM(((HG_HEADS + GLA_HEADS) * CHUNK, SUB * HEAD_DK), BF16),
            pltpu.VMEM((2, (HG_HEADS + GLA_HEADS) * CHUNK, LANES), F32),
        ],
        compiler_params=_cparams(("arbitrary",)),
        name="mixer",
    )(z, ga, lb, g_hg, conv_w, w_gate, b_gate, g_gla)


def _outproj_kernel(y_ref, w_ref, h_ref, o_ref):
    o_ref[...] = h_ref[...] + _dot(y_ref[...], w_ref[...])


def _outproj(y, w, layer, h, tm=1024, tn=1024):
    s = h.shape[0]
    return pl.pallas_call(
        _outproj_kernel,
        grid=(s // tm, D_MODEL // tn),
        in_specs=[
            pl.BlockSpec((tm, D_MODEL), lambda i, j: (i, 0)),
            pl.BlockSpec((None, D_MODEL, tn), lambda i, j: (layer, 0, j)),
            pl.BlockSpec((tm, tn), lambda i, j: (i, j)),
        ],
        out_specs=pl.BlockSpec((tm, tn), lambda i, j: (i, j)),
        out_shape=jax.ShapeDtypeStruct((s, D_MODEL), F32),
        compiler_params=_cparams(("parallel", "arbitrary")),
        name="outproj",
    )(y, w, h)


def _mlp_kernel(h_ref, g_ref, wup_ref, wdown_ref, o_ref, u_ref):
    @pl.when(pl.program_id(1) == 0)
    def _():
        x = h_ref[...]
        u_ref[...] = _rms_scale(x, g_ref[...]).astype(BF16)
        o_ref[...] = x

    m = jnp.maximum(_dot(u_ref[...], wup_ref[...]), 0.0)
    o_ref[...] += _dot((m * m).astype(BF16), wdown_ref[...])


def _mlp(h, g, w_up, w_down, layer, tm=512, tf=1024):
    s = h.shape[0]
    return pl.pallas_call(
        _mlp_kernel,
        grid=(s // tm, D_FF // tf),
        in_specs=[
            pl.BlockSpec((tm, D_MODEL), lambda i, f: (i, 0)),
            pl.BlockSpec((1, D_MODEL), lambda i, f: (0, 0)),
            pl.BlockSpec((None, D_MODEL, tf), lambda i, f: (layer, 0, f)),
            pl.BlockSpec((None, tf, D_MODEL), lambda i, f: (layer, f, 0)),
        ],
        out_specs=pl.BlockSpec((tm, D_MODEL), lambda i, f: (i, 0)),
        out_shape=jax.ShapeDtypeStruct((s, D_MODEL), F32),
        scratch_shapes=[pltpu.VMEM((tm, D_MODEL), BF16)],
        compiler_params=_cparams(("parallel", "arbitrary")),
        name="mlp",
    )(h, g, w_up, w_down)


def _ple_kernel(h_ref, g_ref, wpg_ref, p_ref, wpp_ref, *rest, final):
    x = h_ref[...]
    u = _rms_scale(x, g_ref[...]).astype(BF16)
    gate = jax.nn.sigmoid(_dot(u, wpg_ref[...]))
    y = x + gate * _dot(p_ref[...].astype(BF16), wpp_ref[...])
    if final:
        gf_ref, o_ref = rest
        y = _rms_scale(y, gf_ref[...])
    else:
        (o_ref,) = rest
    o_ref[...] = y


def _ple(h, g, w_pg, p, w_pp, layer, g_final=None, tm=512):
    s = h.shape[0]
    row = lambda i: (i, 0)
    const = lambda i: (0, 0)
    slab = lambda i: (layer, 0, 0)
    in_specs = [
        pl.BlockSpec((tm, D_MODEL), row),
        pl.BlockSpec((1, D_MODEL), const),
        pl.BlockSpec((None, D_MODEL, D_MODEL), slab),
        pl.BlockSpec((None, tm, PLE_DIM), lambda i: (layer, i, 0)),
        pl.BlockSpec((None, PLE_DIM, D_MODEL), slab),
    ]
    args = [h, g, w_pg, p, w_pp]
    if g_final is not None:
        in_specs.append(pl.BlockSpec((1, D_MODEL), const))
        args.append(g_final)
    return pl.pallas_call(
        functools.partial(_ple_kernel, final=g_final is not None),
        grid=(s // tm,),
        in_specs=in_specs,
        out_specs=pl.BlockSpec((tm, D_MODEL), row),
        out_shape=jax.ShapeDtypeStruct((s, D_MODEL), F32),
        compiler_params=_cparams(("parallel",)),
        name="ple",
    )(*args)


def kernel(x, p, g_mix, w_in, lb_logits, g_hg_norm, conv_w, w_gla_gate, b_gla_gate, g_gla_norm,
           w_out, g_mlp, w_up, w_down, g_ple, w_pg, w_pp, g_final):
    batch, seq, _ = x.shape
    depth = w_in.shape[0]
    assert batch == 1 and seq % 1024 == 0
    lb_cum = jnp.cumsum(jax.nn.softmax(lb_logits.astype(F32), axis=0), axis=0)
    lb_all = lb_cum - lb_cum[0:1]
    row = lambda a: a.reshape(1, -1).astype(F32)

    w_in_b = w_in.astype(BF16)
    w_ga = jnp.pad(w_in_b[:, :, IN_MAIN:], ((0, 0), (0, 0), (0, LANES - GLA_GATE_RANK)))
    w_gate = jnp.pad(w_gla_gate, ((0, 0), (0, LANES - GLA_GATE_RANK), (0, 0))).astype(BF16)
    w_out_b, w_up_b, w_down_b = w_out.astype(BF16), w_up.astype(BF16), w_down.astype(BF16)
    w_pg_b, w_pp_b = w_pg.astype(BF16), w_pp.astype(BF16)
    p2 = p.reshape(depth, seq, PLE_DIM)

    h = x.reshape(seq, D_MODEL)
    for l in range(depth):
        z, ga = _inproj(h, row(g_mix[l]), w_in_b, l, w_ga[l])
        y = _mixer(z, ga, row(lb_all[l]), row(g_hg_norm[l]), conv_w[l].astype(F32), w_gate[l],
                   row(b_gla_gate[l]), row(g_gla_norm[l]))
        h = _outproj(y, w_out_b, l, h)
        h = _mlp(h, row(g_mlp[l]), w_up_b, w_down_b, l)
        h = _ple(h, row(g_ple[l]), w_pg_b, p2, w_pp_b, l, row(g_final) if l == depth - 1 else None)
    return h.reshape(batch, seq, D_MODEL)
```

```python
import functools

import jax
import jax.numpy as jnp
from jax import lax
from jax.experimental import pallas as pl
from jax.experimental.pallas import tpu as pltpu

F32 = jnp.float32
BF16 = jnp.bfloat16

EPS = 1e-6
LOG2E = 1.4426950408889634
D_MODEL = 2048
D_FF = 4 * D_MODEL
PLE_DIM = 256
HG_WIDTH = 1024
HG_HEADS = 8
HEAD_DK = 128
HG_DV = 128
GLA_HEADS = 4
GLA_DV = 256
GLA_KEY_WIDTH = GLA_HEADS * HEAD_DK
GLA_WIDTH = GLA_HEADS * GLA_DV
GLA_GATE_RANK = 16
GLA_GATE_NORM = 16.0
CONV_WIDTH = 4
CONV_CH = 2 * GLA_KEY_WIDTH + GLA_WIDTH
IN_MAIN = 4 * HG_WIDTH + CONV_CH + GLA_WIDTH
LANES = 128
SUBLANES = 8

OFF_HQ, OFF_HF, OFF_HI, OFF_HG = 0, HG_WIDTH, 2 * HG_WIDTH, 3 * HG_WIDTH
OFF_CONV = 4 * HG_WIDTH
OFF_GR = OFF_CONV + CONV_CH
DECAY_W = HG_WIDTH + GLA_KEY_WIDTH

CHUNK = 64
SUB = 8
LOG2_SUB = SUB.bit_length() - 1
LOG2_DK = HEAD_DK.bit_length() - 1
MIX_TILE = 256
HALO = SUBLANES

VMEM_LIMIT = 56 * 1024 * 1024


def _cparams(sem):
    return pltpu.CompilerParams(dimension_semantics=sem, vmem_limit_bytes=VMEM_LIMIT)


def _rms_scale(x, g):
    ms = jnp.mean(x * x, axis=-1, keepdims=True)
    return x * lax.rsqrt(ms + EPS) * g


def _dot(a, b):
    return jnp.dot(a, b, preferred_element_type=F32)


def _dot_nt(a, b):
    return lax.dot_general(a, b, (((1,), (1,)), ((), ())), preferred_element_type=F32)


def _dot_tn(a, b):
    return lax.dot_general(a, b, (((0,), (0,)), ((), ())), preferred_element_type=F32)


BF16_ROWS = 2 * SUBLANES


def _cast_rider(w, layer, steps, step_of):
    rows, cols = w.shape[1:]
    per = rows // steps
    if rows % steps == 0 and per % BF16_ROWS == 0:
        repeat = 1
    else:
        per = BF16_ROWS
        assert (steps * per) % rows == 0
        repeat = steps * per // rows
    in_spec = pl.BlockSpec((None, per, cols), lambda *g: (layer, step_of(*g) // repeat, 0))
    out_spec = pl.BlockSpec((None, per, cols), lambda *g: (0, step_of(*g) // repeat, 0))
    return in_spec, out_spec, jax.ShapeDtypeStruct((1, rows, cols), BF16)


def _run_riders(refs):
    n = len(refs) // 2
    for src, dst in zip(refs[:n], refs[n:]):
        dst[...] = src[...].astype(BF16)


def _inproj_kernel(h_ref, g_ref, w_ref, wga_ref, *rest, n_cast):
    cast_src, (z_ref, ga_ref), cast_dst, (u_ref,) = (
        rest[:n_cast], rest[n_cast:n_cast + 2], rest[n_cast + 2:2 * n_cast + 2], rest[2 * n_cast + 2:])
    _run_riders(cast_src + cast_dst)

    @pl.when(pl.program_id(1) == 0)
    def _():
        u_ref[...] = _rms_scale(h_ref[...], g_ref[...]).astype(BF16)
        ga_ref[...] = _dot(u_ref[...], wga_ref[...])

    z_ref[...] = _dot(u_ref[...], w_ref[...]).astype(z_ref.dtype)


def _inproj(h, g, w_in, layer, w_ga, casts=(), tm=1024, tn=1792):
    s = h.shape[0]
    nj = IN_MAIN // tn
    riders = [_cast_rider(w, l, (s // tm) * nj, lambda i, j: i * nj + j) for w, l in casts]
    return pl.pallas_call(
        functools.partial(_inproj_kernel, n_cast=len(casts)),
        grid=(s // tm, nj),
        in_specs=[
            pl.BlockSpec((tm, D_MODEL), lambda i, j: (i, 0)),
            pl.BlockSpec((1, D_MODEL), lambda i, j: (0, 0)),
            pl.BlockSpec((None, D_MODEL, tn), lambda i, j: (layer, 0, j)),
            pl.BlockSpec((D_MODEL, LANES), lambda i, j: (0, 0)),
        ] + [r[0] for r in riders],
        out_specs=[
            pl.BlockSpec((tm, tn), lambda i, j: (i, j)),
            pl.BlockSpec((tm, LANES), lambda i, j: (i, 0)),
        ] + [r[1] for r in riders],
        out_shape=[
            jax.ShapeDtypeStruct((s, IN_MAIN), BF16),
            jax.ShapeDtypeStruct((s, LANES), F32),
        ] + [r[2] for r in riders],
        scratch_shapes=[pltpu.VMEM((tm, D_MODEL), BF16)],
        compiler_params=_cparams(("parallel", "arbitrary")),
        name="inproj",
    )(h, g, w_in, w_ga, *[w for w, _ in casts])


def _log2_1p_exp2_neg_abs(x):
    return jnp.log(1.0 + jnp.exp2(-jnp.abs(x))) * LOG2E


def _log2_sigmoid(x2):
    return jnp.minimum(x2, 0.0) - _log2_1p_exp2_neg_abs(x2)


def _pad_rows(x, start):
    parts = []
    if start:
        parts.append(jnp.zeros((start, x.shape[1]), x.dtype))
    parts.append(x)
    if start + x.shape[0] < CHUNK:
        parts.append(jnp.zeros((CHUNK - start - x.shape[0], x.shape[1]), x.dtype))
    return jnp.concatenate(parts, axis=0) if len(parts) > 1 else x


def _keys_log(keys):
    return keys[0] == "log"


def _diag_terms(q, b_ref, bcol, keys):
    kcol = keys[-1]
    bv = b_ref[pl.ds(HALO, CHUNK), bcol]
    slabs = []
    for d in range(CHUNK // SUB):
        qb = q[d * SUB:(d + 1) * SUB]
        bb = bv[d * SUB:(d + 1) * SUB]
        xs = []
        for j in range(d * SUB, (d + 1) * SUB):
            if _keys_log(keys):
                e = jnp.exp2(jnp.minimum(bb - keys[1][pl.ds(j, 1), kcol],
                                         keys[2][pl.ds(j, 1), kcol]))
            else:
                e = (jnp.exp2(jnp.minimum(bb - b_ref[pl.ds(HALO + j, 1), bcol], 0.0))
                     * keys[1][pl.ds(j, 1), kcol])
            xs.append(qb * e)
        slabs.append(jnp.concatenate(xs, axis=1))
    return jnp.concatenate(slabs, axis=0).astype(BF16)


def _head_chunk(q, v_bf, b_ref, bcol, keys, st_ref, stb_ref, hidx):
    log_keys = _keys_log(keys)
    kcol = keys[-1]

    def brow(j):
        return b_ref[pl.ds(HALO + j, 1), bcol]

    bv = b_ref[pl.ds(HALO, CHUNK), bcol]
    b_prev = brow(-1)
    b_last = brow(CHUNK - 1)
    if log_keys:
        lkv = keys[2][:, kcol]
    else:
        kv = keys[1][:, kcol]

    def q_side(r, lo=0, hi=CHUNK):
        return q[lo:hi] * jnp.exp2(bv[lo:hi] - r)

    def k_side(r, lo=0, hi=CHUNK):
        if log_keys:
            return jnp.exp2(r - bv[lo:hi] + lkv[lo:hi])
        return kv[lo:hi] * jnp.exp2(r - bv[lo:hi])

    o = _dot_nt(q_side(b_prev).astype(BF16), stb_ref[hidx])
    st = st_ref[hidx] * jnp.exp2(b_last - b_prev) + _dot_tn(v_bf, k_side(b_last).astype(BF16))
    st_ref[hidx] = st
    stb_ref[hidx] = st.astype(BF16)

    q_slabs, k_slabs = [], []
    n = SUB
    while n < CHUNK:
        for p in range(CHUNK // (2 * n)):
            left, right = 2 * n * p, 2 * n * p + n
            ref = brow(right - 1)
            q_slabs.append(_pad_rows(q_side(ref, right, right + n), right))
            k_slabs.append(_pad_rows(k_side(ref, left, right), left))
        n *= 2
    a = _dot_nt(jnp.concatenate(q_slabs, axis=1).astype(BF16),
                jnp.concatenate(k_slabs, axis=1).astype(BF16))
    return o, a


def _head_norm_gate(o, gain, gate):
    ms = jnp.mean(o * o, axis=-1, keepdims=True)
    return o * lax.rsqrt(ms + EPS) * gain * (gate * jax.nn.sigmoid(gate))


def _silu(x):
    return x * jax.nn.sigmoid(x)


def _mixer_kernel(z_ref, ga_ref, lb_ref, ghg_ref, convw_ref, wgate_ref, bgate_ref, ggla_ref,
                  y_ref, sth_ref, stg_ref, sthb_ref, stgb_ref, xh_ref, c_ref, lk_ref, gq_ref, gk_ref,
                  gv_ref, b_ref, o_ref, xd_ref, sd_ref):
    tile = y_ref.shape[0]

    @pl.when(pl.program_id(0) == 0)
    def _():
        sth_ref[...] = jnp.zeros_like(sth_ref)
        stg_ref[...] = jnp.zeros_like(stg_ref)
        sthb_ref[...] = jnp.zeros_like(sthb_ref)
        stgb_ref[...] = jnp.zeros_like(stgb_ref)
        xh_ref[pl.ds(0, HALO), :] = jnp.zeros((HALO, CONV_CH), F32)
        b_ref[pl.ds(0, HALO), :] = jnp.zeros((HALO, DECAY_W), F32)

    lb = lb_ref[...]
    l2_lb = jnp.log(lb) * LOG2E
    l2_1m = jnp.log1p(-lb) * LOG2E
    h2 = z_ref[:, OFF_HF:OFF_HF + HG_WIDTH].astype(F32) * LOG2E
    rhs = l2_1m + _log2_sigmoid(h2)
    log2_f = jnp.maximum(l2_lb, rhs) + _log2_1p_exp2_neg_abs(l2_lb - rhs)
    lk = rhs - h2
    lk_ref[...] = lk

    g2 = (_dot(ga_ref[...].astype(BF16), wgate_ref[...]) + bgate_ref[...]) * LOG2E
    log2_alpha = _log2_sigmoid(g2) * (1.0 / GLA_GATE_NORM)

    row = lax.broadcasted_iota(jnp.int32, (tile, tile), 0)
    col = lax.broadcasted_iota(jnp.int32, (tile, tile), 1)
    lag = row - col
    tri = (lag >= 0).astype(BF16)
    logd = jnp.concatenate([log2_f, log2_alpha], axis=1)
    hi = logd.astype(BF16)
    lo = (logd - hi.astype(F32)).astype(BF16)
    bcum = _dot(tri, hi) + _dot(tri, lo)
    b_ref[pl.ds(HALO, tile), :] = bcum
    c_ref[...] = bcum[:, :HG_WIDTH] - lk

    x_bf = z_ref[:, OFF_CONV:OFF_CONV + CONV_CH]
    conv = convw_ref[CONV_WIDTH - 1:CONV_WIDTH, :] * x_bf.astype(F32)
    for j in range(CONV_WIDTH - 1):
        shift = (lag == CONV_WIDTH - 1 - j).astype(BF16)
        conv = conv + convw_ref[j:j + 1, :] * _dot(shift, x_bf)
    xh_ref[pl.ds(HALO, HALO), :] = x_bf[:HALO].astype(F32)
    head = convw_ref[0:1, :] * xh_ref[pl.ds(HALO - CONV_WIDTH + 1, HALO), :]
    for j in range(1, CONV_WIDTH):
        head = head + convw_ref[j:j + 1, :] * xh_ref[pl.ds(HALO - CONV_WIDTH + 1 + j, HALO), :]
    xh_ref[pl.ds(0, HALO), :] = x_bf[tile - HALO:].astype(F32)

    def put_qkv(rows, act):
        gq_ref[rows, :] = act[:, :GLA_KEY_WIDTH] * (HEAD_DK ** -0.5)
        gk_ref[rows, :] = act[:, GLA_KEY_WIDTH:2 * GLA_KEY_WIDTH]
        gv_ref[rows, :] = act[:, 2 * GLA_KEY_WIDTH:].astype(BF16)

    put_qkv(pl.ds(0, tile), _silu(conv))
    put_qkv(pl.ds(0, HALO), _silu(head))

    kk = lax.broadcasted_iota(jnp.int32, (SUB * HEAD_DK, LANES), 0)
    nn = lax.broadcasted_iota(jnp.int32, (SUB * HEAD_DK, LANES), 1)
    e_mat = (jnp.right_shift(kk, LOG2_DK) == (nn & (SUB - 1))).astype(BF16)
    tt = lax.broadcasted_iota(jnp.int32, (CHUNK, CHUNK), 0)
    ss = lax.broadcasted_iota(jnp.int32, (CHUNK, CHUNK), 1)
    m_diag = (jnp.right_shift(tt, LOG2_SUB) == jnp.right_shift(ss, LOG2_SUB)) & (tt >= ss)

    n_heads = HG_HEADS + GLA_HEADS
    group = 4
    lag = 3

    def chunk_heads(r0):
        rows = pl.ds(r0, CHUNK)
        b_view = b_ref.at[pl.ds(r0, HALO + CHUNK)]
        c_view, lk_view, gk_view = c_ref.at[rows], lk_ref.at[rows], gk_ref.at[rows]
        heads = []
        for h in range(HG_HEADS):
            cs = slice(h * HEAD_DK, (h + 1) * HEAD_DK)
            q = z_ref[rows, OFF_HQ + h * HEAD_DK:OFF_HQ + (h + 1) * HEAD_DK].astype(F32)
            v_bf = z_ref[rows, OFF_HI + h * HG_DV:OFF_HI + (h + 1) * HG_DV]
            heads.append((q, v_bf, cs, ("log", c_view, lk_view, cs), sth_ref, sthb_ref, h,
                          slice(h * HG_DV, (h + 1) * HG_DV)))
        for h in range(GLA_HEADS):
            ks = slice(h * HEAD_DK, (h + 1) * HEAD_DK)
            vs = slice(h * GLA_DV, (h + 1) * GLA_DV)
            bs = slice(HG_WIDTH + h * HEAD_DK, HG_WIDTH + (h + 1) * HEAD_DK)
            heads.append((gq_ref[rows, ks], gv_ref[rows, vs], bs, ("lin", gk_view, ks), stg_ref, stgb_ref,
                          h, slice(HG_WIDTH + h * GLA_DV, HG_WIDTH + (h + 1) * GLA_DV)))
        return rows, b_view, heads

    def diag_group(chunk, g, slot):
        _, b_view, heads = chunk
        for i in range(g * group, (g + 1) * group):
            q, _, bcol, keys = heads[i][:4]
            xd_ref[pl.ds(i * CHUNK, CHUNK), :] = _diag_terms(q, b_view, bcol, keys)
        grows = pl.ds(g * group * CHUNK, group * CHUNK)
        sd_ref[slot, grows, :] = _dot(xd_ref[grows, :], e_mat)

    def rest_group(chunk, g, slot, pending):
        rows, b_view, heads = chunk
        for i in range(g * group, (g + 1) * group):
            q, v_bf, bcol, keys, st_ref, stb_ref, hidx, ocol = heads[i]
            o, a = _head_chunk(q, v_bf, b_view, bcol, keys, st_ref, stb_ref, hidx)
            pending = pending + [(rows, slot, i, o, a, v_bf, ocol)]
            if len(pending) > lag:
                finish(*pending[0])
                pending = pending[1:]
        return pending

    def finish(rows, slot, i, o, a, v_bf, ocol):
        sd = sd_ref[slot, pl.ds(i * CHUNK, CHUNK), :]
        a = jnp.where(m_diag, sd[:, :CHUNK], a)
        o_ref[rows, ocol] = o + _dot(a.astype(BF16), v_bf)

    n_chunks = tile // CHUNK
    first = chunk_heads(0)
    for g in range(n_heads // group):
        diag_group(first, g, 0)

    def chunk_body(c, carry):
        cur = chunk_heads(pl.multiple_of(c * CHUNK, CHUNK))
        nxt = chunk_heads(pl.multiple_of((c + 1) * CHUNK, CHUNK))
        slot = c & 1
        pending = []
        for g in range(n_heads // group):
            diag_group(nxt, g, 1 - slot)
            pending = rest_group(cur, g, slot, pending)
        for item in pending:
            finish(*item)
        return carry

    lax.fori_loop(0, n_chunks - 1, chunk_body, 0)
    last = chunk_heads((n_chunks - 1) * CHUNK)
    pending = []
    for g in range(n_heads // group):
        pending = rest_group(last, g, (n_chunks - 1) & 1, pending)
    for item in pending:
        finish(*item)

    for h in range(HG_HEADS):
        cs = slice(h * HG_DV, (h + 1) * HG_DV)
        gate = z_ref[:, OFF_HG + h * HG_DV:OFF_HG + (h + 1) * HG_DV].astype(F32)
        y_ref[:, cs] = _head_norm_gate(o_ref[:, cs], ghg_ref[:, cs], gate).astype(y_ref.dtype)
    for h in range(GLA_HEADS):
        vs = slice(h * GLA_DV, (h + 1) * GLA_DV)
        gate = z_ref[:, OFF_GR + h * GLA_DV:OFF_GR + (h + 1) * GLA_DV].astype(F32)
        y_ref[:, HG_WIDTH + h * GLA_DV:HG_WIDTH + (h + 1) * GLA_DV] = _head_norm_gate(
            o_ref[:, HG_WIDTH + h * GLA_DV:HG_WIDTH + (h + 1) * GLA_DV], ggla_ref[:, vs],
            gate).astype(y_ref.dtype)


def _mixer(z, ga, lb, g_hg, conv_w, w_gate, b_gate, g_gla, tile=MIX_TILE):
    s = z.shape[0]
    const = lambda i: (0, 0)
    return pl.pallas_call(
        _mixer_kernel,
        grid=(s // tile,),
        in_specs=[
            pl.BlockSpec((tile, IN_MAIN), lambda i: (i, 0)),
            pl.BlockSpec((tile, LANES), lambda i: (i, 0)),
            pl.BlockSpec((1, HG_WIDTH), const),
            pl.BlockSpec((1, HG_WIDTH), const),
            pl.BlockSpec((CONV_WIDTH, CONV_CH), const),
            pl.BlockSpec((LANES, GLA_KEY_WIDTH), const),
            pl.BlockSpec((1, GLA_KEY_WIDTH), const),
            pl.BlockSpec((1, GLA_WIDTH), const),
        ],
        out_specs=pl.BlockSpec((tile, D_MODEL), lambda i: (i, 0)),
        out_shape=jax.ShapeDtypeStruct((s, D_MODEL), BF16),
        scratch_shapes=[
            pltpu.VMEM((HG_HEADS, HG_DV, HEAD_DK), F32),
            pltpu.VMEM((GLA_HEADS, GLA_DV, HEAD_DK), F32),
            pltpu.VMEM((HG_HEADS, HG_DV, HEAD_DK), BF16),
            pltpu.VMEM((GLA_HEADS, GLA_DV, HEAD_DK), BF16),
            pltpu.VMEM((2 * HALO, CONV_CH), F32),
            pltpu.VMEM((tile, HG_WIDTH), F32),
            pltpu.VMEM((tile, HG_WIDTH), F32),
            pltpu.VMEM((tile, GLA_KEY_WIDTH), F32),
            pltpu.VMEM((tile, GLA_KEY_WIDTH), F32),
            pltpu.VMEM((tile, GLA_WIDTH), BF16),
            pltpu.VMEM((HALO + tile, DECAY_W), F32),
            pltpu.VMEM((tile, D_MODEL), F32),
            pltpu.VMEM(((HG_HEADS + GLA_HEADS) * CHUNK, SUB * HEAD_DK), BF16),
            pltpu.VMEM((2, (HG_HEADS + GLA_HEADS) * CHUNK, LANES), F32),
        ],
        compiler_params=_cparams(("arbitrary",)),
        name="mixer",
    )(z, ga, lb, g_hg, conv_w, w_gate, b_gate, g_gla)


def _outproj_kernel(y_ref, w_ref, h_ref, o_ref):
    o_ref[...] = h_ref[...] + _dot(y_ref[...], w_ref[...])


def _outproj(y, w, layer, h, tm=1024, tn=1024):
    s = h.shape[0]
    return pl.pallas_call(
        _outproj_kernel,
        grid=(s // tm, D_MODEL // tn),
        in_specs=[
            pl.BlockSpec((tm, D_MODEL), lambda i, j: (i, 0)),
            pl.BlockSpec((None, D_MODEL, tn), lambda i, j: (layer, 0, j)),
            pl.BlockSpec((tm, tn), lambda i, j: (i, j)),
        ],
        out_specs=pl.BlockSpec((tm, tn), lambda i, j: (i, j)),
        out_shape=jax.ShapeDtypeStruct((s, D_MODEL), F32),
        compiler_params=_cparams(("parallel", "arbitrary")),
        name="outproj",
    )(y, w, h)


def _mlp_kernel(h_ref, g_ref, wup_ref, wdown_ref, *rest, n_cast):
    cast_src, o_ref, cast_dst, u_ref = (
        rest[:n_cast], rest[n_cast], rest[n_cast + 1:2 * n_cast + 1], rest[2 * n_cast + 1])
    _run_riders(cast_src + cast_dst)

    @pl.when(pl.program_id(1) == 0)
    def _():
        x = h_ref[...]
        u_ref[...] = _rms_scale(x, g_ref[...]).astype(BF16)
        o_ref[...] = x

    m = jnp.maximum(_dot(u_ref[...], wup_ref[...]), 0.0)
    o_ref[...] += _dot((m * m).astype(BF16), wdown_ref[...])


def _mlp(h, g, w_up, w_down, layer, casts=(), tm=512, tf=1024):
    s = h.shape[0]
    nf = D_FF // tf
    riders = [_cast_rider(w, l, (s // tm) * nf, lambda i, f: i * nf + f) for w, l in casts]
    return pl.pallas_call(
        functools.partial(_mlp_kernel, n_cast=len(casts)),
        grid=(s // tm, nf),
        in_specs=[
            pl.BlockSpec((tm, D_MODEL), lambda i, f: (i, 0)),
            pl.BlockSpec((1, D_MODEL), lambda i, f: (0, 0)),
            pl.BlockSpec((None, D_MODEL, tf), lambda i, f: (layer, 0, f)),
            pl.BlockSpec((None, tf, D_MODEL), lambda i, f: (layer, f, 0)),
        ] + [r[0] for r in riders],
        out_specs=[pl.BlockSpec((tm, D_MODEL), lambda i, f: (i, 0))] + [r[1] for r in riders],
        out_shape=[jax.ShapeDtypeStruct((s, D_MODEL), F32)] + [r[2] for r in riders],
        scratch_shapes=[pltpu.VMEM((tm, D_MODEL), BF16)],
        compiler_params=_cparams(("parallel", "arbitrary")),
        name="mlp",
    )(h, g, w_up, w_down, *[w for w, _ in casts])


def _ple_kernel(h_ref, g_ref, wpg_ref, p_ref, wpp_ref, *rest, final):
    x = h_ref[...]
    u = _rms_scale(x, g_ref[...]).astype(BF16)
    gate = jax.nn.sigmoid(_dot(u, wpg_ref[...]))
    y = x + gate * _dot(p_ref[...].astype(BF16), wpp_ref[...])
    if final:
        gf_ref, o_ref = rest
        y = _rms_scale(y, gf_ref[...])
    else:
        (o_ref,) = rest
    o_ref[...] = y


def _ple(h, g, w_pg, p, w_pp, layer, g_final=None, tm=512):
    s = h.shape[0]
    row = lambda i: (i, 0)
    const = lambda i: (0, 0)
    slab = lambda i: (layer, 0, 0)
    in_specs = [
        pl.BlockSpec((tm, D_MODEL), row),
        pl.BlockSpec((1, D_MODEL), const),
        pl.BlockSpec((None, D_MODEL, D_MODEL), slab),
        pl.BlockSpec((None, tm, PLE_DIM), lambda i: (layer, i, 0)),
        pl.BlockSpec((None, PLE_DIM, D_MODEL), slab),
    ]
    args = [h, g, w_pg, p, w_pp]
    if g_final is not None:
        in_specs.append(pl.BlockSpec((1, D_MODEL), const))
        args.append(g_final)
    return pl.pallas_call(
        functools.partial(_ple_kernel, final=g_final is not None),
        grid=(s // tm,),
        in_specs=in_specs,
        out_specs=pl.BlockSpec((tm, D_MODEL), row),
        out_shape=jax.ShapeDtypeStruct((s, D_MODEL), F32),
        compiler_params=_cparams(("parallel",)),
        name="ple",
    )(*args)


def kernel(x, p, g_mix, w_in, lb_logits, g_hg_norm, conv_w, w_gla_gate, b_gla_gate, g_gla_norm,
           w_out, g_mlp, w_up, w_down, g_ple, w_pg, w_pp, g_final):
    batch, seq, _ = x.shape
    depth = w_in.shape[0]
    assert batch == 1 and seq % 1024 == 0
    lb_cum = jnp.cumsum(jax.nn.softmax(lb_logits.astype(F32), axis=0), axis=0)
    lb_all = lb_cum - lb_cum[0:1]
    row = lambda a: a.reshape(1, -1).astype(F32)

    w_in_b, w_up_b, w_down_b = (w[0:1].astype(BF16) for w in (w_in, w_up, w_down))
    w_gate = jnp.pad(w_gla_gate, ((0, 0), (0, LANES - GLA_GATE_RANK), (0, 0))).astype(BF16)
    w_out_b, w_pg_b, w_pp_b = w_out.astype(BF16), w_pg.astype(BF16), w_pp.astype(BF16)
    p2 = p.reshape(depth, seq, PLE_DIM)

    h = x.reshape(seq, D_MODEL)
    for l in range(depth):
        ahead = l + 1 < depth
        w_ga = jnp.pad(w_in_b[0, :, IN_MAIN:], ((0, 0), (0, LANES - GLA_GATE_RANK)))
        z, ga, *w_in_next = _inproj(h, row(g_mix[l]), w_in_b, 0, w_ga,
                                    casts=((w_in, l + 1),) if ahead else ())
        y = _mixer(z, ga, row(lb_all[l]), row(g_hg_norm[l]), conv_w[l].astype(F32), w_gate[l],
                   row(b_gla_gate[l]), row(g_gla_norm[l]))
        h = _outproj(y, w_out_b, l, h)
        h, *w_mlp_next = _mlp(h, row(g_mlp[l]), w_up_b, w_down_b, 0,
                              casts=((w_up, l + 1), (w_down, l + 1)) if ahead else ())
        if ahead:
            (w_in_b,), (w_up_b, w_down_b) = w_in_next, w_mlp_next
        h = _ple(h, row(g_ple[l]), w_pg_b, p2, w_pp_b, l, row(g_final) if l == depth - 1 else None)
    return h.reshape(batch, seq, D_MODEL)
```

```python
import functools

import jax
import jax.numpy as jnp
from jax import lax
from jax.experimental import pallas as pl
from jax.experimental.pallas import tpu as pltpu

F32 = jnp.float32
BF16 = jnp.bfloat16

EPS = 1e-6
LOG2E = 1.4426950408889634
D_MODEL = 2048
D_FF = 4 * D_MODEL
PLE_DIM = 256
HG_WIDTH = 1024
HG_HEADS = 8
HEAD_DK = 128
HG_DV = 128
GLA_HEADS = 4
GLA_DV = 256
GLA_KEY_WIDTH = GLA_HEADS * HEAD_DK
GLA_WIDTH = GLA_HEADS * GLA_DV
GLA_GATE_RANK = 16
GLA_GATE_NORM = 16.0
CONV_WIDTH = 4
CONV_CH = 2 * GLA_KEY_WIDTH + GLA_WIDTH
IN_MAIN = 4 * HG_WIDTH + CONV_CH + GLA_WIDTH
LANES = 128
SUBLANES = 8

OFF_HQ, OFF_HF, OFF_HI, OFF_HG = 0, HG_WIDTH, 2 * HG_WIDTH, 3 * HG_WIDTH
OFF_CONV = 4 * HG_WIDTH
OFF_GR = OFF_CONV + CONV_CH
DECAY_W = HG_WIDTH + GLA_KEY_WIDTH

CHUNK = 64
SUB = 8
LOG2_SUB = SUB.bit_length() - 1
LOG2_DK = HEAD_DK.bit_length() - 1
MIX_TILE = 256
HALO = SUBLANES

VMEM_LIMIT = 56 * 1024 * 1024


def _cparams(sem):
    return pltpu.CompilerParams(dimension_semantics=sem, vmem_limit_bytes=VMEM_LIMIT)


def _rms_scale(x, g):
    ms = jnp.mean(x * x, axis=-1, keepdims=True)
    return x * lax.rsqrt(ms + EPS) * g


def _dot(a, b):
    return jnp.dot(a, b, preferred_element_type=F32)


def _dot_nt(a, b):
    return lax.dot_general(a, b, (((1,), (1,)), ((), ())), preferred_element_type=F32)


def _dot_tn(a, b):
    return lax.dot_general(a, b, (((0,), (0,)), ((), ())), preferred_element_type=F32)


def _inproj_kernel(h_ref, g_ref, w_ref, wga_ref, z_ref, ga_ref, u_ref):
    @pl.when(pl.program_id(1) == 0)
    def _():
        u_ref[...] = _rms_scale(h_ref[...], g_ref[...]).astype(BF16)
        ga_ref[...] = _dot(u_ref[...], wga_ref[...])

    z_ref[...] = _dot(u_ref[...], w_ref[...]).astype(z_ref.dtype)


def _inproj(h, g, w_in, layer, w_ga, tm=1024, tn=1792):
    s = h.shape[0]
    return pl.pallas_call(
        _inproj_kernel,
        grid=(s // tm, IN_MAIN // tn),
        in_specs=[
            pl.BlockSpec((tm, D_MODEL), lambda i, j: (i, 0)),
            pl.BlockSpec((1, D_MODEL), lambda i, j: (0, 0)),
            pl.BlockSpec((None, D_MODEL, tn), lambda i, j: (layer, 0, j)),
            pl.BlockSpec((D_MODEL, LANES), lambda i, j: (0, 0)),
        ],
        out_specs=[
            pl.BlockSpec((tm, tn), lambda i, j: (i, j)),
            pl.BlockSpec((tm, LANES), lambda i, j: (i, 0)),
        ],
        out_shape=[
            jax.ShapeDtypeStruct((s, IN_MAIN), BF16),
            jax.ShapeDtypeStruct((s, LANES), F32),
        ],
        scratch_shapes=[pltpu.VMEM((tm, D_MODEL), BF16)],
        compiler_params=_cparams(("parallel", "arbitrary")),
        name="inproj",
    )(h, g, w_in, w_ga)


def _log2_1p_exp2_neg_abs(x):
    return jnp.log(1.0 + jnp.exp2(-jnp.abs(x))) * LOG2E


def _log2_sigmoid(x2):
    return jnp.minimum(x2, 0.0) - _log2_1p_exp2_neg_abs(x2)


def _pad_rows(x, start):
    parts = []
    if start:
        parts.append(jnp.zeros((start, x.shape[1]), x.dtype))
    parts.append(x)
    if start + x.shape[0] < CHUNK:
        parts.append(jnp.zeros((CHUNK - start - x.shape[0], x.shape[1]), x.dtype))
    return jnp.concatenate(parts, axis=0) if len(parts) > 1 else x


def _keys_log(keys):
    return keys[0] == "log"


def _diag_terms(q, b_ref, bcol, keys):
    kcol = keys[-1]
    bv = b_ref[pl.ds(HALO, CHUNK), bcol]
    slabs = []
    for d in range(CHUNK // SUB):
        qb = q[d * SUB:(d + 1) * SUB]
        bb = bv[d * SUB:(d + 1) * SUB]
        xs = []
        for j in range(d * SUB, (d + 1) * SUB):
            if _keys_log(keys):
                e = jnp.exp2(jnp.minimum(bb - keys[1][pl.ds(j, 1), kcol],
                                         keys[2][pl.ds(j, 1), kcol]))
            else:
                e = (jnp.exp2(jnp.minimum(bb - b_ref[pl.ds(HALO + j, 1), bcol], 0.0))
                     * keys[1][pl.ds(j, 1), kcol])
            xs.append(qb * e)
        slabs.append(jnp.concatenate(xs, axis=1))
    return jnp.concatenate(slabs, axis=0).astype(BF16)


def _head_chunk(q, v_bf, b_ref, bcol, keys, st_ref, stb_ref, hidx, m_pair):
    log_keys = _keys_log(keys)
    kcol = keys[-1]

    def brow(j):
        return b_ref[pl.ds(HALO + j, 1), bcol]

    bv = b_ref[pl.ds(HALO, CHUNK), bcol]
    b_prev = brow(-1)
    b_last = brow(CHUNK - 1)
    if log_keys:
        lkv = keys[2][:, kcol]
    else:
        kv = keys[1][:, kcol]

    def q_side(r, lo=0, hi=CHUNK):
        return q[lo:hi] * jnp.exp2(bv[lo:hi] - r)

    def k_side(r, lo=0, hi=CHUNK):
        if log_keys:
            return jnp.exp2(r - bv[lo:hi] + lkv[lo:hi])
        return kv[lo:hi] * jnp.exp2(r - bv[lo:hi])

    o = _dot_nt(q_side(b_prev).astype(BF16), stb_ref[hidx])
    st = st_ref[hidx] * jnp.exp2(b_last - b_prev) + _dot_tn(v_bf, k_side(b_last).astype(BF16))
    st_ref[hidx] = st
    stb_ref[hidx] = st.astype(BF16)

    q_slabs, k_slabs = [], []
    n = 2 * SUB
    while n < CHUNK:
        for p in range(CHUNK // (2 * n)):
            left, right = 2 * n * p, 2 * n * p + n
            ref = brow(right - 1)
            q_slabs.append(_pad_rows(q_side(ref, right, right + n), right))
            k_slabs.append(_pad_rows(k_side(ref, left, right), left))
        n *= 2
    a = _dot_nt(jnp.concatenate(q_slabs, axis=1).astype(BF16),
                jnp.concatenate(k_slabs, axis=1).astype(BF16))
    q1, k1 = [], []
    for p in range(CHUNK // (2 * SUB)):
        left, right = 2 * SUB * p, 2 * SUB * p + SUB
        ref = brow(right - 1)
        zeros = jnp.zeros((SUB, HEAD_DK), F32)
        q1 += [zeros, q_side(ref, right, right + SUB)]
        k1 += [k_side(ref, left, right), zeros]
    a1 = _dot_nt(jnp.concatenate(q1, axis=0).astype(BF16), jnp.concatenate(k1, axis=0).astype(BF16))
    a = jnp.where(m_pair, a1, a)
    return o, a


def _head_norm_gate(o, gain, gate):
    ms = jnp.mean(o * o, axis=-1, keepdims=True)
    return o * lax.rsqrt(ms + EPS) * gain * (gate * jax.nn.sigmoid(gate))


def _silu(x):
    return x * jax.nn.sigmoid(x)


def _mixer_kernel(z_ref, ga_ref, lb_ref, ghg_ref, convw_ref, wgate_ref, bgate_ref, ggla_ref,
                  y_ref, sth_ref, stg_ref, sthb_ref, stgb_ref, xh_ref, c_ref, lk_ref, gq_ref, gk_ref,
                  gv_ref, b_ref, o_ref, xd_ref, sd_ref):
    tile = y_ref.shape[0]

    @pl.when(pl.program_id(0) == 0)
    def _():
        sth_ref[...] = jnp.zeros_like(sth_ref)
        stg_ref[...] = jnp.zeros_like(stg_ref)
        sthb_ref[...] = jnp.zeros_like(sthb_ref)
        stgb_ref[...] = jnp.zeros_like(stgb_ref)
        xh_ref[pl.ds(0, HALO), :] = jnp.zeros((HALO, CONV_CH), F32)
        b_ref[pl.ds(0, HALO), :] = jnp.zeros((HALO, DECAY_W), F32)

    lb = lb_ref[...]
    l2_lb = jnp.log(lb) * LOG2E
    l2_1m = jnp.log1p(-lb) * LOG2E
    h2 = z_ref[:, OFF_HF:OFF_HF + HG_WIDTH].astype(F32) * LOG2E
    rhs = l2_1m + _log2_sigmoid(h2)
    log2_f = jnp.maximum(l2_lb, rhs) + _log2_1p_exp2_neg_abs(l2_lb - rhs)
    lk = rhs - h2
    lk_ref[...] = lk

    g2 = (_dot(ga_ref[...].astype(BF16), wgate_ref[...]) + bgate_ref[...]) * LOG2E
    log2_alpha = _log2_sigmoid(g2) * (1.0 / GLA_GATE_NORM)

    row = lax.broadcasted_iota(jnp.int32, (tile, tile), 0)
    col = lax.broadcasted_iota(jnp.int32, (tile, tile), 1)
    lag = row - col
    tri = (lag >= 0).astype(BF16)
    logd = jnp.concatenate([log2_f, log2_alpha], axis=1)
    hi = logd.astype(BF16)
    lo = (logd - hi.astype(F32)).astype(BF16)
    bcum = _dot(tri, hi) + _dot(tri, lo)
    b_ref[pl.ds(HALO, tile), :] = bcum
    c_ref[...] = bcum[:, :HG_WIDTH] - lk

    x_bf = z_ref[:, OFF_CONV:OFF_CONV + CONV_CH]
    conv = convw_ref[CONV_WIDTH - 1:CONV_WIDTH, :] * x_bf.astype(F32)
    for j in range(CONV_WIDTH - 1):
        shift = (lag == CONV_WIDTH - 1 - j).astype(BF16)
        conv = conv + convw_ref[j:j + 1, :] * _dot(shift, x_bf)
    xh_ref[pl.ds(HALO, HALO), :] = x_bf[:HALO].astype(F32)
    head = convw_ref[0:1, :] * xh_ref[pl.ds(HALO - CONV_WIDTH + 1, HALO), :]
    for j in range(1, CONV_WIDTH):
        head = head + convw_ref[j:j + 1, :] * xh_ref[pl.ds(HALO - CONV_WIDTH + 1 + j, HALO), :]
    xh_ref[pl.ds(0, HALO), :] = x_bf[tile - HALO:].astype(F32)

    def put_qkv(rows, act):
        gq_ref[rows, :] = act[:, :GLA_KEY_WIDTH] * (HEAD_DK ** -0.5)
        gk_ref[rows, :] = act[:, GLA_KEY_WIDTH:2 * GLA_KEY_WIDTH]
        gv_ref[rows, :] = act[:, 2 * GLA_KEY_WIDTH:].astype(BF16)

    put_qkv(pl.ds(0, tile), _silu(conv))
    put_qkv(pl.ds(0, HALO), _silu(head))

    kk = lax.broadcasted_iota(jnp.int32, (SUB * HEAD_DK, LANES), 0)
    nn = lax.broadcasted_iota(jnp.int32, (SUB * HEAD_DK, LANES), 1)
    e_mat = (jnp.right_shift(kk, LOG2_DK) == (nn & (SUB - 1))).astype(BF16)
    tt = lax.broadcasted_iota(jnp.int32, (CHUNK, CHUNK), 0)
    ss = lax.broadcasted_iota(jnp.int32, (CHUNK, CHUNK), 1)
    m_diag = (jnp.right_shift(tt, LOG2_SUB) == jnp.right_shift(ss, LOG2_SUB)) & (tt >= ss)
    m_pair = (jnp.right_shift(tt, LOG2_SUB + 1) == jnp.right_shift(ss, LOG2_SUB + 1)) & (tt >= ss)

    n_heads = HG_HEADS + GLA_HEADS
    group = 4
    lag = 3

    def chunk_heads(r0):
        rows = pl.ds(r0, CHUNK)
        b_view = b_ref.at[pl.ds(r0, HALO + CHUNK)]
        c_view, lk_view, gk_view = c_ref.at[rows], lk_ref.at[rows], gk_ref.at[rows]
        heads = []
        for h in range(HG_HEADS):
            cs = slice(h * HEAD_DK, (h + 1) * HEAD_DK)
            q = z_ref[rows, OFF_HQ + h * HEAD_DK:OFF_HQ + (h + 1) * HEAD_DK].astype(F32)
            v_bf = z_ref[rows, OFF_HI + h * HG_DV:OFF_HI + (h + 1) * HG_DV]
            heads.append((q, v_bf, cs, ("log", c_view, lk_view, cs), sth_ref, sthb_ref, h,
                          slice(h * HG_DV, (h + 1) * HG_DV)))
        for h in range(GLA_HEADS):
            ks = slice(h * HEAD_DK, (h + 1) * HEAD_DK)
            vs = slice(h * GLA_DV, (h + 1) * GLA_DV)
            bs = slice(HG_WIDTH + h * HEAD_DK, HG_WIDTH + (h + 1) * HEAD_DK)
            heads.append((gq_ref[rows, ks], gv_ref[rows, vs], bs, ("lin", gk_view, ks), stg_ref, stgb_ref,
                          h, slice(HG_WIDTH + h * GLA_DV, HG_WIDTH + (h + 1) * GLA_DV)))
        return rows, b_view, heads

    def diag_group(chunk, g, slot):
        _, b_view, heads = chunk
        for i in range(g * group, (g + 1) * group):
            q, _, bcol, keys = heads[i][:4]
            xd_ref[pl.ds(i * CHUNK, CHUNK), :] = _diag_terms(q, b_view, bcol, keys)
        grows = pl.ds(g * group * CHUNK, group * CHUNK)
        sd_ref[slot, grows, :] = _dot(xd_ref[grows, :], e_mat)

    def rest_group(chunk, g, slot, pending):
        rows, b_view, heads = chunk
        for i in range(g * group, (g + 1) * group):
            q, v_bf, bcol, keys, st_ref, stb_ref, hidx, ocol = heads[i]
            o, a = _head_chunk(q, v_bf, b_view, bcol, keys, st_ref, stb_ref, hidx, m_pair)
            pending = pending + [(rows, slot, i, o, a, v_bf, ocol)]
            if len(pending) > lag:
                finish(*pending[0])
                pending = pending[1:]
        return pending

    def finish(rows, slot, i, o, a, v_bf, ocol):
        sd = sd_ref[slot, pl.ds(i * CHUNK, CHUNK), :]
        a = jnp.where(m_diag, sd[:, :CHUNK], a)
        o_ref[rows, ocol] = o + _dot(a.astype(BF16), v_bf)

    n_chunks = tile // CHUNK
    first = chunk_heads(0)
    for g in range(n_heads // group):
        diag_group(first, g, 0)

    def chunk_body(c, carry):
        cur = chunk_heads(pl.multiple_of(c * CHUNK, CHUNK))
        nxt = chunk_heads(pl.multiple_of((c + 1) * CHUNK, CHUNK))
        slot = c & 1
        pending = []
        for g in range(n_heads // group):
            diag_group(nxt, g, 1 - slot)
            pending = rest_group(cur, g, slot, pending)
        for item in pending:
            finish(*item)
        return carry

    lax.fori_loop(0, n_chunks - 1, chunk_body, 0)
    last = chunk_heads((n_chunks - 1) * CHUNK)
    pending = []
    for g in range(n_heads // group):
        pending = rest_group(last, g, (n_chunks - 1) & 1, pending)
    for item in pending:
        finish(*item)

    for h in range(HG_HEADS):
        cs = slice(h * HG_DV, (h + 1) * HG_DV)
        gate = z_ref[:, OFF_HG + h * HG_DV:OFF_HG + (h + 1) * HG_DV].astype(F32)
        y_ref[:, cs] = _head_norm_gate(o_ref[:, cs], ghg_ref[:, cs], gate).astype(y_ref.dtype)
    for h in range(GLA_HEADS):
        vs = slice(h * GLA_DV, (h + 1) * GLA_DV)
        gate = z_ref[:, OFF_GR + h * GLA_DV:OFF_GR + (h + 1) * GLA_DV].astype(F32)
        y_ref[:, HG_WIDTH + h * GLA_DV:HG_WIDTH + (h + 1) * GLA_DV] = _head_norm_gate(
            o_ref[:, HG_WIDTH + h * GLA_DV:HG_WIDTH + (h + 1) * GLA_DV], ggla_ref[:, vs],
            gate).astype(y_ref.dtype)


def _mixer(z, ga, lb, g_hg, conv_w, w_gate, b_gate, g_gla, tile=MIX_TILE):
    s = z.shape[0]
    const = lambda i: (0, 0)
    return pl.pallas_call(
        _mixer_kernel,
        grid=(s // tile,),
        in_specs=[
            pl.BlockSpec((tile, IN_MAIN), lambda i: (i, 0)),
            pl.BlockSpec((tile, LANES), lambda i: (i, 0)),
            pl.BlockSpec((1, HG_WIDTH), const),
            pl.BlockSpec((1, HG_WIDTH), const),
            pl.BlockSpec((CONV_WIDTH, CONV_CH), const),
            pl.BlockSpec((LANES, GLA_KEY_WIDTH), const),
            pl.BlockSpec((1, GLA_KEY_WIDTH), const),
            pl.BlockSpec((1, GLA_WIDTH), const),
        ],
        out_specs=pl.BlockSpec((tile, D_MODEL), lambda i: (i, 0)),
        out_shape=jax.ShapeDtypeStruct((s, D_MODEL), BF16),
        scratch_shapes=[
            pltpu.VMEM((HG_HEADS, HG_DV, HEAD_DK), F32),
            pltpu.VMEM((GLA_HEADS, GLA_DV, HEAD_DK), F32),
            pltpu.VMEM((HG_HEADS, HG_DV, HEAD_DK), BF16),
            pltpu.VMEM((GLA_HEADS, GLA_DV, HEAD_DK), BF16),
            pltpu.VMEM((2 * HALO, CONV_CH), F32),
            pltpu.VMEM((tile, HG_WIDTH), F32),
            pltpu.VMEM((tile, HG_WIDTH), F32),
            pltpu.VMEM((tile, GLA_KEY_WIDTH), F32),
            pltpu.VMEM((tile, GLA_KEY_WIDTH), F32),
            pltpu.VMEM((tile, GLA_WIDTH), BF16),
            pltpu.VMEM((HALO + tile, DECAY_W), F32),
            pltpu.VMEM((tile, D_MODEL), F32),
            pltpu.VMEM(((HG_HEADS + GLA_HEADS) * CHUNK, SUB * HEAD_DK), BF16),
            pltpu.VMEM((2, (HG_HEADS + GLA_HEADS) * CHUNK, LANES), F32),
        ],
        compiler_params=_cparams(("arbitrary",)),
        name="mixer",
    )(z, ga, lb, g_hg, conv_w, w_gate, b_gate, g_gla)


def _outproj_kernel(y_ref, w_ref, h_ref, o_ref):
    o_ref[...] = h_ref[...] + _dot(y_ref[...], w_ref[...])


def _outproj(y, w, layer, h, tm=512):
    s = h.shape[0]
    row = lambda i: (i, 0)
    return pl.pallas_call(
        _outproj_kernel,
        grid=(s // tm,),
        in_specs=[
            pl.BlockSpec((tm, D_MODEL), row),
            pl.BlockSpec((None, D_MODEL, D_MODEL), lambda i: (layer, 0, 0)),
            pl.BlockSpec((tm, D_MODEL), row),
        ],
        out_specs=pl.BlockSpec((tm, D_MODEL), row),
        out_shape=jax.ShapeDtypeStruct((s, D_MODEL), F32),
        compiler_params=_cparams(("parallel",)),
        name="outproj",
    )(y, w, h)


def _mlp_kernel(h_ref, g_ref, wup_ref, wdown_ref, o_ref, u_ref):
    @pl.when(pl.program_id(1) == 0)
    def _():
        x = h_ref[...]
        u_ref[...] = _rms_scale(x, g_ref[...]).astype(BF16)
        o_ref[...] = x

    m = jnp.maximum(_dot(u_ref[...], wup_ref[...]), 0.0)
    o_ref[...] += _dot((m * m).astype(BF16), wdown_ref[...])


def _mlp(h, g, w_up, w_down, layer, tm=512, tf=1024):
    s = h.shape[0]
    return pl.pallas_call(
        _mlp_kernel,
        grid=(s // tm, D_FF // tf),
        in_specs=[
            pl.BlockSpec((tm, D_MODEL), lambda i, f: (i, 0)),
            pl.BlockSpec((1, D_MODEL), lambda i, f: (0, 0)),
            pl.BlockSpec((None, D_MODEL, tf), lambda i, f: (layer, 0, f)),
            pl.BlockSpec((None, tf, D_MODEL), lambda i, f: (layer, f, 0)),
        ],
        out_specs=pl.BlockSpec((tm, D_MODEL), lambda i, f: (i, 0)),
        out_shape=jax.ShapeDtypeStruct((s, D_MODEL), F32),
        scratch_shapes=[pltpu.VMEM((tm, D_MODEL), BF16)],
        compiler_params=_cparams(("parallel", "arbitrary")),
        name="mlp",
    )(h, g, w_up, w_down)


def _ple_kernel(h_ref, g_ref, wpg_ref, p_ref, wpp_ref, *rest, final):
    x = h_ref[...]
    u = _rms_scale(x, g_ref[...]).astype(BF16)
    gate = jax.nn.sigmoid(_dot(u, wpg_ref[...]))
    y = x + gate * _dot(p_ref[...].astype(BF16), wpp_ref[...])
    if final:
        gf_ref, o_ref = rest
        y = _rms_scale(y, gf_ref[...])
    else:
        (o_ref,) = rest
    o_ref[...] = y


def _ple(h, g, w_pg, p, w_pp, layer, g_final=None, tm=512):
    s = h.shape[0]
    row = lambda i: (i, 0)
    const = lambda i: (0, 0)
    slab = lambda i: (layer, 0, 0)
    in_specs = [
        pl.BlockSpec((tm, D_MODEL), row),
        pl.BlockSpec((1, D_MODEL), const),
        pl.BlockSpec((None, D_MODEL, D_MODEL), slab),
        pl.BlockSpec((None, tm, PLE_DIM), lambda i: (layer, i, 0)),
        pl.BlockSpec((None, PLE_DIM, D_MODEL), slab),
    ]
    args = [h, g, w_pg, p, w_pp]
    if g_final is not None:
        in_specs.append(pl.BlockSpec((1, D_MODEL), const))
        args.append(g_final)
    return pl.pallas_call(
        functools.partial(_ple_kernel, final=g_final is not None),
        grid=(s // tm,),
        in_specs=in_specs,
        out_specs=pl.BlockSpec((tm, D_MODEL), row),
        out_shape=jax.ShapeDtypeStruct((s, D_MODEL), F32),
        compiler_params=_cparams(("parallel",)),
        name="ple",
    )(*args)


def kernel(x, p, g_mix, w_in, lb_logits, g_hg_norm, conv_w, w_gla_gate, b_gla_gate, g_gla_norm,
           w_out, g_mlp, w_up, w_down, g_ple, w_pg, w_pp, g_final):
    batch, seq, _ = x.shape
    depth = w_in.shape[0]
    assert batch == 1 and seq % 1024 == 0
    lb_cum = jnp.cumsum(jax.nn.softmax(lb_logits.astype(F32), axis=0), axis=0)
    lb_all = lb_cum - lb_cum[0:1]
    row = lambda a: a.reshape(1, -1).astype(F32)

    w_in_b = w_in.astype(BF16)
    w_ga = jnp.pad(w_in_b[:, :, IN_MAIN:], ((0, 0), (0, 0), (0, LANES - GLA_GATE_RANK)))
    w_gate = jnp.pad(w_gla_gate, ((0, 0), (0, LANES - GLA_GATE_RANK), (0, 0))).astype(BF16)
    w_out_b, w_up_b, w_down_b = w_out.astype(BF16), w_up.astype(BF16), w_down.astype(BF16)
    w_pg_b, w_pp_b = w_pg.astype(BF16), w_pp.astype(BF16)
    p2 = p.reshape(depth, seq, PLE_DIM)

    h = x.reshape(seq, D_MODEL)
    for l in range(depth):
        z, ga = _inproj(h, row(g_mix[l]), w_in_b, l, w_ga[l])
        y = _mixer(z, ga, row(lb_all[l]), row(g_hg_norm[l]), conv_w[l].astype(F32), w_gate[l],
                   row(b_gla_gate[l]), row(g_gla_norm[l]))
        h = _outproj(y, w_out_b, l, h)
        h = _mlp(h, row(g_mlp[l]), w_up_b, w_down_b, l)
        h = _ple(h, row(g_ple[l]), w_pg_b, p2, w_pp_b, l, row(g_final) if l == depth - 1 else None)
    return h.reshape(batch, seq, D_MODEL)
```

```python
import functools

import jax
import jax.numpy as jnp
from jax import lax
from jax.experimental import pallas as pl
from jax.experimental.pallas import tpu as pltpu

F32 = jnp.float32
BF16 = jnp.bfloat16

EPS = 1e-6
LOG2E = 1.4426950408889634
D_MODEL = 2048
D_FF = 4 * D_MODEL
PLE_DIM = 256
HG_WIDTH = 1024
HG_HEADS = 8
HEAD_DK = 128
HG_DV = 128
GLA_HEADS = 4
GLA_DV = 256
GLA_KEY_WIDTH = GLA_HEADS * HEAD_DK
GLA_WIDTH = GLA_HEADS * GLA_DV
GLA_GATE_RANK = 16
GLA_GATE_NORM = 16.0
CONV_WIDTH = 4
CONV_CH = 2 * GLA_KEY_WIDTH + GLA_WIDTH
IN_MAIN = 4 * HG_WIDTH + CONV_CH + GLA_WIDTH
LANES = 128
SUBLANES = 8

OFF_HQ, OFF_HF, OFF_HI, OFF_HG = 0, HG_WIDTH, 2 * HG_WIDTH, 3 * HG_WIDTH
OFF_CONV = 4 * HG_WIDTH
OFF_GR = OFF_CONV + CONV_CH
DECAY_W = HG_WIDTH + GLA_KEY_WIDTH

CHUNK = 64
SUB = 8
LOG2_SUB = SUB.bit_length() - 1
MIX_TILE = 256
HALO = SUBLANES

VMEM_LIMIT = 56 * 1024 * 1024


def _cparams(sem):
    return pltpu.CompilerParams(dimension_semantics=sem, vmem_limit_bytes=VMEM_LIMIT)


def _rms_scale(x, g):
    ms = jnp.mean(x * x, axis=-1, keepdims=True)
    return x * lax.rsqrt(ms + EPS) * g


def _dot(a, b):
    return jnp.dot(a, b, preferred_element_type=F32)


def _dot_nt(a, b):
    return lax.dot_general(a, b, (((1,), (1,)), ((), ())), preferred_element_type=F32)


def _dot_tn(a, b):
    return lax.dot_general(a, b, (((0,), (0,)), ((), ())), preferred_element_type=F32)


def _cast_kernel(w_ref, o_ref):
    o_ref[...] = w_ref[...].astype(BF16)


def _cast_bf16(w, rows=256):
    depth, r, c = w.shape
    spec = pl.BlockSpec((None, rows, c), lambda l, i: (l, i, 0))
    return pl.pallas_call(
        _cast_kernel,
        grid=(depth, r // rows),
        in_specs=[spec],
        out_specs=spec,
        out_shape=jax.ShapeDtypeStruct(w.shape, BF16),
        compiler_params=_cparams(("parallel", "parallel")),
        name="cast",
    )(w)


def _inproj_kernel(h_ref, g_ref, w_ref, wga_ref, z_ref, ga_ref, u_ref):
    @pl.when(pl.program_id(1) == 0)
    def _():
        u_ref[...] = _rms_scale(h_ref[...], g_ref[...]).astype(BF16)
        ga_ref[...] = _dot(u_ref[...], wga_ref[...])

    z_ref[...] = _dot(u_ref[...], w_ref[...]).astype(z_ref.dtype)


def _inproj(h, g, w_in, layer, w_ga, tm=1024, tn=1792):
    s = h.shape[0]
    return pl.pallas_call(
        _inproj_kernel,
        grid=(s // tm, IN_MAIN // tn),
        in_specs=[
            pl.BlockSpec((tm, D_MODEL), lambda i, j: (i, 0)),
            pl.BlockSpec((1, D_MODEL), lambda i, j: (0, 0)),
            pl.BlockSpec((None, D_MODEL, tn), lambda i, j: (layer, 0, j)),
            pl.BlockSpec((D_MODEL, LANES), lambda i, j: (0, 0)),
        ],
        out_specs=[
            pl.BlockSpec((tm, tn), lambda i, j: (i, j)),
            pl.BlockSpec((tm, LANES), lambda i, j: (i, 0)),
        ],
        out_shape=[
            jax.ShapeDtypeStruct((s, IN_MAIN), BF16),
            jax.ShapeDtypeStruct((s, LANES), F32),
        ],
        scratch_shapes=[pltpu.VMEM((tm, D_MODEL), BF16)],
        compiler_params=_cparams(("parallel", "arbitrary")),
        name="inproj",
    )(h, g, w_in, w_ga)


def _log2_1p_exp2_neg_abs(x):
    return jnp.log(1.0 + jnp.exp2(-jnp.abs(x))) * LOG2E


def _log2_sigmoid(x2):
    return jnp.minimum(x2, 0.0) - _log2_1p_exp2_neg_abs(x2)


def _pad_rows(x, start):
    parts = []
    if start:
        parts.append(jnp.zeros((start, x.shape[1]), x.dtype))
    parts.append(x)
    if start + x.shape[0] < CHUNK:
        parts.append(jnp.zeros((CHUNK - start - x.shape[0], x.shape[1]), x.dtype))
    return jnp.concatenate(parts, axis=0) if len(parts) > 1 else x


def _keys_log(keys):
    return keys[0] == "log"


def _diag_terms(q, b_ref, bcol, keys):
    kcol = keys[-1]
    bv = b_ref[pl.ds(HALO, CHUNK), bcol]
    slabs = []
    for d in range(CHUNK // SUB):
        qb = q[d * SUB:(d + 1) * SUB]
        bb = bv[d * SUB:(d + 1) * SUB]
        xs = []
        for j in range(d * SUB, (d + 1) * SUB):
            if _keys_log(keys):
                e = jnp.exp2(jnp.minimum(bb - keys[1][pl.ds(j, 1), kcol],
                                         keys[2][pl.ds(j, 1), kcol]))
            else:
                e = (jnp.exp2(jnp.minimum(bb - b_ref[pl.ds(HALO + j, 1), bcol], 0.0))
                     * keys[1][pl.ds(j, 1), kcol])
            xs.append(qb * e)
        slabs.append(jnp.concatenate(xs, axis=1))
    return jnp.concatenate(slabs, axis=0).astype(BF16)


def _head_chunk(q, v_bf, b_ref, bcol, keys, st_ref, stb_ref, hidx, m_pair):
    log_keys = _keys_log(keys)
    kcol = keys[-1]

    def brow(j):
        return b_ref[pl.ds(HALO + j, 1), bcol]

    bv = b_ref[pl.ds(HALO, CHUNK), bcol]
    b_prev = brow(-1)
    b_last = brow(CHUNK - 1)
    if log_keys:
        lkv = keys[2][:, kcol]
    else:
        kv = keys[1][:, kcol]

    def q_side(r, lo=0, hi=CHUNK):
        return q[lo:hi] * jnp.exp2(bv[lo:hi] - r)

    def k_side(r, lo=0, hi=CHUNK):
        if log_keys:
            return jnp.exp2(r - bv[lo:hi] + lkv[lo:hi])
        return kv[lo:hi] * jnp.exp2(r - bv[lo:hi])

    o = _dot_nt(q_side(b_prev).astype(BF16), stb_ref[hidx])
    st = st_ref[hidx] * jnp.exp2(b_last - b_prev) + _dot_tn(v_bf, k_side(b_last).astype(BF16))
    st_ref[hidx] = st
    stb_ref[hidx] = st.astype(BF16)

    q_slabs, k_slabs = [], []
    n = 2 * SUB
    while n < CHUNK:
        for p in range(CHUNK // (2 * n)):
            left, right = 2 * n * p, 2 * n * p + n
            ref = brow(right - 1)
            q_slabs.append(_pad_rows(q_side(ref, right, right + n), right))
            k_slabs.append(_pad_rows(k_side(ref, left, right), left))
        n *= 2
    a = _dot_nt(jnp.concatenate(q_slabs, axis=1).astype(BF16),
                jnp.concatenate(k_slabs, axis=1).astype(BF16))
    q1, k1 = [], []
    for p in range(CHUNK // (2 * SUB)):
        left, right = 2 * SUB * p, 2 * SUB * p + SUB
        ref = brow(right - 1)
        zeros = jnp.zeros((SUB, HEAD_DK), F32)
        q1 += [zeros, q_side(ref, right, right + SUB)]
        k1 += [k_side(ref, left, right), zeros]
    a1 = _dot_nt(jnp.concatenate(q1, axis=0).astype(BF16), jnp.concatenate(k1, axis=0).astype(BF16))
    a = jnp.where(m_pair, a1, a)
    return o, a


def _head_norm_gate(o, gain, gate):
    ms = jnp.mean(o * o, axis=-1, keepdims=True)
    return o * lax.rsqrt(ms + EPS) * gain * (gate * jax.nn.sigmoid(gate))


def _silu(x):
    return x * jax.nn.sigmoid(x)


def _mixer_kernel(z_ref, ga_ref, lb_ref, ghg_ref, convw_ref, wgate_ref, bgate_ref, ggla_ref, rowops_ref,
                  emat_ref, y_ref, sth_ref, stg_ref, sthb_ref, stgb_ref, xh_ref, c_ref, lk_ref, gq_ref, gk_ref,
                  gv_ref, b_ref, o_ref, xd_ref, sd_ref):
    tile = y_ref.shape[0]

    @pl.when(pl.program_id(0) == 0)
    def _():
        sth_ref[...] = jnp.zeros_like(sth_ref)
        stg_ref[...] = jnp.zeros_like(stg_ref)
        sthb_ref[...] = jnp.zeros_like(sthb_ref)
        stgb_ref[...] = jnp.zeros_like(stgb_ref)
        xh_ref[pl.ds(0, HALO), :] = jnp.zeros((HALO, CONV_CH), F32)
        b_ref[pl.ds(0, HALO), :] = jnp.zeros((HALO, DECAY_W), F32)

    lb = lb_ref[...]
    l2_lb = jnp.log(lb) * LOG2E
    l2_1m = jnp.log1p(-lb) * LOG2E
    h2 = z_ref[:, OFF_HF:OFF_HF + HG_WIDTH].astype(F32) * LOG2E
    rhs = l2_1m + _log2_sigmoid(h2)
    log2_f = jnp.maximum(l2_lb, rhs) + _log2_1p_exp2_neg_abs(l2_lb - rhs)
    lk = rhs - h2
    lk_ref[...] = lk

    g2 = (_dot(ga_ref[...].astype(BF16), wgate_ref[...]) + bgate_ref[...]) * LOG2E
    log2_alpha = _log2_sigmoid(g2) * (1.0 / GLA_GATE_NORM)

    tri = rowops_ref[0]
    logd = jnp.concatenate([log2_f, log2_alpha], axis=1)
    hi = logd.astype(BF16)
    lo = (logd - hi.astype(F32)).astype(BF16)
    bcum = _dot(tri, hi) + _dot(tri, lo)
    b_ref[pl.ds(HALO, tile), :] = bcum
    c_ref[...] = bcum[:, :HG_WIDTH] - lk

    x_bf = z_ref[:, OFF_CONV:OFF_CONV + CONV_CH]
    conv = convw_ref[CONV_WIDTH - 1:CONV_WIDTH, :] * x_bf.astype(F32)
    for j in range(CONV_WIDTH - 1):
        conv = conv + convw_ref[j:j + 1, :] * _dot(rowops_ref[CONV_WIDTH - 1 - j], x_bf)
    xh_ref[pl.ds(HALO, HALO), :] = x_bf[:HALO].astype(F32)
    head = convw_ref[0:1, :] * xh_ref[pl.ds(HALO - CONV_WIDTH + 1, HALO), :]
    for j in range(1, CONV_WIDTH):
        head = head + convw_ref[j:j + 1, :] * xh_ref[pl.ds(HALO - CONV_WIDTH + 1 + j, HALO), :]
    xh_ref[pl.ds(0, HALO), :] = x_bf[tile - HALO:].astype(F32)

    def put_qkv(rows, act):
        gq_ref[rows, :] = act[:, :GLA_KEY_WIDTH] * (HEAD_DK ** -0.5)
        gk_ref[rows, :] = act[:, GLA_KEY_WIDTH:2 * GLA_KEY_WIDTH]
        gv_ref[rows, :] = act[:, 2 * GLA_KEY_WIDTH:].astype(BF16)

    put_qkv(pl.ds(0, tile), _silu(conv))
    put_qkv(pl.ds(0, HALO), _silu(head))

    e_mat = emat_ref[...]
    tt = lax.broadcasted_iota(jnp.int32, (CHUNK, CHUNK), 0)
    ss = lax.broadcasted_iota(jnp.int32, (CHUNK, CHUNK), 1)
    m_diag = (jnp.right_shift(tt, LOG2_SUB) == jnp.right_shift(ss, LOG2_SUB)) & (tt >= ss)
    m_pair = (jnp.right_shift(tt, LOG2_SUB + 1) == jnp.right_shift(ss, LOG2_SUB + 1)) & (tt >= ss)

    n_heads = HG_HEADS + GLA_HEADS
    group = 4
    lag = 3

    def chunk_heads(r0):
        rows = pl.ds(r0, CHUNK)
        b_view = b_ref.at[pl.ds(r0, HALO + CHUNK)]
        c_view, lk_view, gk_view = c_ref.at[rows], lk_ref.at[rows], gk_ref.at[rows]
        heads = []
        for h in range(HG_HEADS):
            cs = slice(h * HEAD_DK, (h + 1) * HEAD_DK)
            q = z_ref[rows, OFF_HQ + h * HEAD_DK:OFF_HQ + (h + 1) * HEAD_DK].astype(F32)
            v_bf = z_ref[rows, OFF_HI + h * HG_DV:OFF_HI + (h + 1) * HG_DV]
            heads.append((q, v_bf, cs, ("log", c_view, lk_view, cs), sth_ref, sthb_ref, h,
                          slice(h * HG_DV, (h + 1) * HG_DV)))
        for h in range(GLA_HEADS):
            ks = slice(h * HEAD_DK, (h + 1) * HEAD_DK)
            vs = slice(h * GLA_DV, (h + 1) * GLA_DV)
            bs = slice(HG_WIDTH + h * HEAD_DK, HG_WIDTH + (h + 1) * HEAD_DK)
            heads.append((gq_ref[rows, ks], gv_ref[rows, vs], bs, ("lin", gk_view, ks), stg_ref, stgb_ref,
                          h, slice(HG_WIDTH + h * GLA_DV, HG_WIDTH + (h + 1) * GLA_DV)))
        return rows, b_view, heads

    def diag_group(chunk, g, slot):
        _, b_view, heads = chunk
        for i in range(g * group, (g + 1) * group):
            q, _, bcol, keys = heads[i][:4]
            xd_ref[pl.ds(i * CHUNK, CHUNK), :] = _diag_terms(q, b_view, bcol, keys)
        grows = pl.ds(g * group * CHUNK, group * CHUNK)
        sd_ref[slot, grows, :] = _dot(xd_ref[grows, :], e_mat)

    def rest_group(chunk, g, slot, pending):
        rows, b_view, heads = chunk
        for i in range(g * group, (g + 1) * group):
            q, v_bf, bcol, keys, st_ref, stb_ref, hidx, ocol = heads[i]
            o, a = _head_chunk(q, v_bf, b_view, bcol, keys, st_ref, stb_ref, hidx, m_pair)
            pending = pending + [(rows, slot, i, o, a, v_bf, ocol)]
            if len(pending) > lag:
                finish(*pending[0])
                pending = pending[1:]
        return pending

    def finish(rows, slot, i, o, a, v_bf, ocol):
        sd = sd_ref[slot, pl.ds(i * CHUNK, CHUNK), :]
        a = jnp.where(m_diag, sd[:, :CHUNK], a)
        o_ref[rows, ocol] = o + _dot(a.astype(BF16), v_bf)

    n_chunks = tile // CHUNK
    first = chunk_heads(0)
    for g in range(n_heads // group):
        diag_group(first, g, 0)

    def chunk_body(c, carry):
        cur = chunk_heads(pl.multiple_of(c * CHUNK, CHUNK))
        nxt = chunk_heads(pl.multiple_of((c + 1) * CHUNK, CHUNK))
        slot = c & 1
        pending = []
        for g in range(n_heads // group):
            diag_group(nxt, g, 1 - slot)
            pending = rest_group(cur, g, slot, pending)
        for item in pending:
            finish(*item)
        return carry

    lax.fori_loop(0, n_chunks - 1, chunk_body, 0)
    last = chunk_heads((n_chunks - 1) * CHUNK)
    pending = []
    for g in range(n_heads // group):
        pending = rest_group(last, g, (n_chunks - 1) & 1, pending)
    for item in pending:
        finish(*item)

    for h in range(HG_HEADS):
        cs = slice(h * HG_DV, (h + 1) * HG_DV)
        gate = z_ref[:, OFF_HG + h * HG_DV:OFF_HG + (h + 1) * HG_DV].astype(F32)
        y_ref[:, cs] = _head_norm_gate(o_ref[:, cs], ghg_ref[:, cs], gate).astype(y_ref.dtype)
    for h in range(GLA_HEADS):
        vs = slice(h * GLA_DV, (h + 1) * GLA_DV)
        gate = z_ref[:, OFF_GR + h * GLA_DV:OFF_GR + (h + 1) * GLA_DV].astype(F32)
        y_ref[:, HG_WIDTH + h * GLA_DV:HG_WIDTH + (h + 1) * GLA_DV] = _head_norm_gate(
            o_ref[:, HG_WIDTH + h * GLA_DV:HG_WIDTH + (h + 1) * GLA_DV], ggla_ref[:, vs],
            gate).astype(y_ref.dtype)


def _mixer(z, ga, lb, g_hg, conv_w, w_gate, b_gate, g_gla, tile=MIX_TILE):
    s = z.shape[0]
    const = lambda i: (0, 0)
    lag = jnp.arange(tile)[:, None] - jnp.arange(tile)[None, :]
    rowops = jnp.stack([lag >= 0] + [lag == d for d in range(1, CONV_WIDTH)]).astype(BF16)
    e_mat = ((jnp.arange(SUB * HEAD_DK)[:, None] // HEAD_DK) == (jnp.arange(LANES)[None, :] % SUB)).astype(BF16)
    return pl.pallas_call(
        _mixer_kernel,
        grid=(s // tile,),
        in_specs=[
            pl.BlockSpec((tile, IN_MAIN), lambda i: (i, 0)),
            pl.BlockSpec((tile, LANES), lambda i: (i, 0)),
            pl.BlockSpec((1, HG_WIDTH), const),
            pl.BlockSpec((1, HG_WIDTH), const),
            pl.BlockSpec((CONV_WIDTH, CONV_CH), const),
            pl.BlockSpec((LANES, GLA_KEY_WIDTH), const),
            pl.BlockSpec((1, GLA_KEY_WIDTH), const),
            pl.BlockSpec((1, GLA_WIDTH), const),
            pl.BlockSpec((CONV_WIDTH, tile, tile), lambda i: (0, 0, 0)),
            pl.BlockSpec((SUB * HEAD_DK, LANES), const),
        ],
        out_specs=pl.BlockSpec((tile, D_MODEL), lambda i: (i, 0)),
        out_shape=jax.ShapeDtypeStruct((s, D_MODEL), BF16),
        scratch_shapes=[
            pltpu.VMEM((HG_HEADS, HG_DV, HEAD_DK), F32),
            pltpu.VMEM((GLA_HEADS, GLA_DV, HEAD_DK), F32),
            pltpu.VMEM((HG_HEADS, HG_DV, HEAD_DK), BF16),
            pltpu.VMEM((GLA_HEADS, GLA_DV, HEAD_DK), BF16),
            pltpu.VMEM((2 * HALO, CONV_CH), F32),
            pltpu.VMEM((tile, HG_WIDTH), F32),
            pltpu.VMEM((tile, HG_WIDTH), F32),
            pltpu.VMEM((tile, GLA_KEY_WIDTH), F32),
            pltpu.VMEM((tile, GLA_KEY_WIDTH), F32),
            pltpu.VMEM((tile, GLA_WIDTH), BF16),
            pltpu.VMEM((HALO + tile, DECAY_W), F32),
            pltpu.VMEM((tile, D_MODEL), F32),
            pltpu.VMEM(((HG_HEADS + GLA_HEADS) * CHUNK, SUB * HEAD_DK), BF16),
            pltpu.VMEM((2, (HG_HEADS + GLA_HEADS) * CHUNK, LANES), F32),
        ],
        compiler_params=_cparams(("arbitrary",)),
        name="mixer",
    )(z, ga, lb, g_hg, conv_w, w_gate, b_gate, g_gla, rowops, e_mat)


def _outproj_kernel(y_ref, w_ref, h_ref, o_ref):
    o_ref[...] = h_ref[...] + _dot(y_ref[...], w_ref[...])


def _outproj(y, w, layer, h, tm=512):
    s = h.shape[0]
    row = lambda i: (i, 0)
    return pl.pallas_call(
        _outproj_kernel,
        grid=(s // tm,),
        in_specs=[
            pl.BlockSpec((tm, D_MODEL), row),
            pl.BlockSpec((None, D_MODEL, D_MODEL), lambda i: (layer, 0, 0)),
            pl.BlockSpec((tm, D_MODEL), row),
        ],
        out_specs=pl.BlockSpec((tm, D_MODEL), row),
        out_shape=jax.ShapeDtypeStruct((s, D_MODEL), F32),
        compiler_params=_cparams(("parallel",)),
        name="outproj",
    )(y, w, h)


def _mlp_kernel(h_ref, g_ref, wup_ref, wdown_ref, o_ref, u_ref):
    @pl.when(pl.program_id(1) == 0)
    def _():
        x = h_ref[...]
        u_ref[...] = _rms_scale(x, g_ref[...]).astype(BF16)
        o_ref[...] = x

    m = jnp.maximum(_dot(u_ref[...], wup_ref[...]), 0.0)
    o_ref[...] += _dot((m * m).astype(BF16), wdown_ref[...])


def _mlp(h, g, w_up, w_down, layer, tm=512, tf=1024):
    s = h.shape[0]
    return pl.pallas_call(
        _mlp_kernel,
        grid=(s // tm, D_FF // tf),
        in_specs=[
            pl.BlockSpec((tm, D_MODEL), lambda i, f: (i, 0)),
            pl.BlockSpec((1, D_MODEL), lambda i, f: (0, 0)),
            pl.BlockSpec((None, D_MODEL, tf), lambda i, f: (layer, 0, f)),
            pl.BlockSpec((None, tf, D_MODEL), lambda i, f: (layer, f, 0)),
        ],
        out_specs=pl.BlockSpec((tm, D_MODEL), lambda i, f: (i, 0)),
        out_shape=jax.ShapeDtypeStruct((s, D_MODEL), F32),
        scratch_shapes=[pltpu.VMEM((tm, D_MODEL), BF16)],
        compiler_params=_cparams(("parallel", "arbitrary")),
        name="mlp",
    )(h, g, w_up, w_down)


def _ple_kernel(h_ref, g_ref, wpg_ref, p_ref, wpp_ref, *rest, final):
    x = h_ref[...]
    u = _rms_scale(x, g_ref[...]).astype(BF16)
    gate = jax.nn.sigmoid(_dot(u, wpg_ref[...]))
    y = x + gate * _dot(p_ref[...].astype(BF16), wpp_ref[...])
    if final:
        gf_ref, o_ref = rest
        y = _rms_scale(y, gf_ref[...])
    else:
        (o_ref,) = rest
    o_ref[...] = y


def _ple(h, g, w_pg, p, w_pp, layer, g_final=None, tm=512):
    s = h.shape[0]
    row = lambda i: (i, 0)
    const = lambda i: (0, 0)
    slab = lambda i: (layer, 0, 0)
    in_specs = [
        pl.BlockSpec((tm, D_MODEL), row),
        pl.BlockSpec((1, D_MODEL), const),
        pl.BlockSpec((None, D_MODEL, D_MODEL), slab),
        pl.BlockSpec((None, tm, PLE_DIM), lambda i: (layer, i, 0)),
        pl.BlockSpec((None, PLE_DIM, D_MODEL), slab),
    ]
    args = [h, g, w_pg, p, w_pp]
    if g_final is not None:
        in_specs.append(pl.BlockSpec((1, D_MODEL), const))
        args.append(g_final)
    return pl.pallas_call(
        functools.partial(_ple_kernel, final=g_final is not None),
        grid=(s // tm,),
        in_specs=in_specs,
        out_specs=pl.BlockSpec((tm, D_MODEL), row),
        out_shape=jax.ShapeDtypeStruct((s, D_MODEL), F32),
        compiler_params=_cparams(("parallel",)),
        name="ple",
    )(*args)


def kernel(x, p, g_mix, w_in, lb_logits, g_hg_norm, conv_w, w_gla_gate, b_gla_gate, g_gla_norm,
           w_out, g_mlp, w_up, w_down, g_ple, w_pg, w_pp, g_final):
    batch, seq, _ = x.shape
    depth = w_in.shape[0]
    assert batch == 1 and seq % 1024 == 0
    lb_cum = jnp.cumsum(jax.nn.softmax(lb_logits.astype(F32), axis=0), axis=0)
    lb_all = lb_cum - lb_cum[0:1]
    row = lambda a: a.reshape(1, -1).astype(F32)

    w_in_b = _cast_bf16(w_in)
    w_ga = jnp.pad(w_in_b[:, :, IN_MAIN:], ((0, 0), (0, 0), (0, LANES - GLA_GATE_RANK)))
    w_gate = jnp.pad(w_gla_gate, ((0, 0), (0, LANES - GLA_GATE_RANK), (0, 0))).astype(BF16)
    w_out_b, w_up_b, w_down_b = w_out.astype(BF16), w_up.astype(BF16), w_down.astype(BF16)
    w_pg_b, w_pp_b = w_pg.astype(BF16), w_pp.astype(BF16)
    p2 = p.reshape(depth, seq, PLE_DIM)

    h = x.reshape(seq, D_MODEL)
    for l in range(depth):
        z, ga = _inproj(h, row(g_mix[l]), w_in_b, l, w_ga[l])
        y = _mixer(z, ga, row(lb_all[l]), row(g_hg_norm[l]), conv_w[l].astype(F32), w_gate[l],
                   row(b_gla_gate[l]), row(g_gla_norm[l]))
        h = _outproj(y, w_out_b, l, h)
        h = _mlp(h, row(g_mlp[l]), w_up_b, w_down_b, l)
        h = _ple(h, row(g_ple[l]), w_pg_b, p2, w_pp_b, l, row(g_final) if l == depth - 1 else None)
    return h.reshape(batch, seq, D_MODEL)
```

```python
import functools

import jax
import jax.numpy as jnp
from jax import lax
from jax.experimental import pallas as pl
from jax.experimental.pallas import tpu as pltpu

F32 = jnp.float32
BF16 = jnp.bfloat16

EPS = 1e-6
LOG2E = 1.4426950408889634
D_MODEL = 2048
D_FF = 4 * D_MODEL
PLE_DIM = 256
HG_WIDTH = 1024
HG_HEADS = 8
HEAD_DK = 128
HG_DV = 128
GLA_HEADS = 4
GLA_DV = 256
GLA_KEY_WIDTH = GLA_HEADS * HEAD_DK
GLA_WIDTH = GLA_HEADS * GLA_DV
GLA_GATE_RANK = 16
GLA_GATE_NORM = 16.0
CONV_WIDTH = 4
CONV_CH = 2 * GLA_KEY_WIDTH + GLA_WIDTH
IN_MAIN = 4 * HG_WIDTH + CONV_CH + GLA_WIDTH
LANES = 128
SUBLANES = 8

OFF_HQ, OFF_HF, OFF_HI, OFF_HG = 0, HG_WIDTH, 2 * HG_WIDTH, 3 * HG_WIDTH
OFF_CONV = 4 * HG_WIDTH
OFF_GR = OFF_CONV + CONV_CH
DECAY_W = HG_WIDTH + GLA_KEY_WIDTH

CHUNK = 64
SUB = 8
LOG2_SUB = SUB.bit_length() - 1
MIX_TILE = 256
HALO = SUBLANES

VMEM_LIMIT = 56 * 1024 * 1024


def _cparams(sem):
    return pltpu.CompilerParams(dimension_semantics=sem, vmem_limit_bytes=VMEM_LIMIT)


def _rms_scale(x, g):
    ms = jnp.mean(x * x, axis=-1, keepdims=True)
    return x * lax.rsqrt(ms + EPS) * g


def _dot(a, b):
    return jnp.dot(a, b, preferred_element_type=F32)


def _dot_nt(a, b):
    return lax.dot_general(a, b, (((1,), (1,)), ((), ())), preferred_element_type=F32)


def _dot_tn(a, b):
    return lax.dot_general(a, b, (((0,), (0,)), ((), ())), preferred_element_type=F32)


def _inproj_kernel(h_ref, g_ref, w_ref, wga_ref, z_ref, ga_ref, u_ref):
    @pl.when(pl.program_id(1) == 0)
    def _():
        u_ref[...] = _rms_scale(h_ref[...], g_ref[...]).astype(BF16)
        ga_ref[...] = _dot(u_ref[...], wga_ref[...])

    z_ref[...] = _dot(u_ref[...], w_ref[...]).astype(z_ref.dtype)


def _inproj(h, g, w_in, layer, w_ga, tm=1024, tn=1792):
    s = h.shape[0]
    return pl.pallas_call(
        _inproj_kernel,
        grid=(s // tm, IN_MAIN // tn),
        in_specs=[
            pl.BlockSpec((tm, D_MODEL), lambda i, j: (i, 0)),
            pl.BlockSpec((1, D_MODEL), lambda i, j: (0, 0)),
            pl.BlockSpec((None, D_MODEL, tn), lambda i, j: (layer, 0, j)),
            pl.BlockSpec((D_MODEL, LANES), lambda i, j: (0, 0)),
        ],
        out_specs=[
            pl.BlockSpec((tm, tn), lambda i, j: (i, j)),
            pl.BlockSpec((tm, LANES), lambda i, j: (i, 0)),
        ],
        out_shape=[
            jax.ShapeDtypeStruct((s, IN_MAIN), BF16),
            jax.ShapeDtypeStruct((s, LANES), F32),
        ],
        scratch_shapes=[pltpu.VMEM((tm, D_MODEL), BF16)],
        compiler_params=_cparams(("parallel", "arbitrary")),
        name="inproj",
    )(h, g, w_in, w_ga)


def _log2_1p_exp2_neg_abs(x):
    return jnp.log(1.0 + jnp.exp2(-jnp.abs(x))) * LOG2E


def _log2_sigmoid(x2):
    return jnp.minimum(x2, 0.0) - _log2_1p_exp2_neg_abs(x2)


def _pad_rows(x, start):
    parts = []
    if start:
        parts.append(jnp.zeros((start, x.shape[1]), x.dtype))
    parts.append(x)
    if start + x.shape[0] < CHUNK:
        parts.append(jnp.zeros((CHUNK - start - x.shape[0], x.shape[1]), x.dtype))
    return jnp.concatenate(parts, axis=0) if len(parts) > 1 else x


def _keys_log(keys):
    return keys[0] == "log"


def _diag_terms(q, b_ref, bcol, keys):
    kcol = keys[-1]
    bv = b_ref[pl.ds(HALO, CHUNK), bcol]
    slabs = []
    for d in range(CHUNK // SUB):
        qb = q[d * SUB:(d + 1) * SUB]
        bb = bv[d * SUB:(d + 1) * SUB]
        xs = []
        for j in range(d * SUB, (d + 1) * SUB):
            if _keys_log(keys):
                e = jnp.exp2(jnp.minimum(bb - keys[1][pl.ds(j, 1), kcol],
                                         keys[2][pl.ds(j, 1), kcol]))
            else:
                e = (jnp.exp2(jnp.minimum(bb - b_ref[pl.ds(HALO + j, 1), bcol], 0.0))
                     * keys[1][pl.ds(j, 1), kcol])
            xs.append(qb * e)
        slabs.append(jnp.concatenate(xs, axis=1))
    return jnp.concatenate(slabs, axis=0).astype(BF16)


def _head_chunk(q, v_bf, b_ref, bcol, keys, st_ref, stb_ref, hidx, m_pair):
    log_keys = _keys_log(keys)
    kcol = keys[-1]

    def brow(j):
        return b_ref[pl.ds(HALO + j, 1), bcol]

    bv = b_ref[pl.ds(HALO, CHUNK), bcol]
    b_prev = brow(-1)
    b_last = brow(CHUNK - 1)
    if log_keys:
        lkv = keys[2][:, kcol]
    else:
        kv = keys[1][:, kcol]

    def q_side(r, lo=0, hi=CHUNK):
        return q[lo:hi] * jnp.exp2(bv[lo:hi] - r)

    def k_side(r, lo=0, hi=CHUNK):
        if log_keys:
            return jnp.exp2(r - bv[lo:hi] + lkv[lo:hi])
        return kv[lo:hi] * jnp.exp2(r - bv[lo:hi])

    o = _dot_nt(q_side(b_prev).astype(BF16), stb_ref[hidx])
    st = st_ref[hidx] * jnp.exp2(b_last - b_prev) + _dot_tn(v_bf, k_side(b_last).astype(BF16))
    st_ref[hidx] = st
    stb_ref[hidx] = st.astype(BF16)

    q_slabs, k_slabs = [], []
    n = 2 * SUB
    while n < CHUNK:
        for p in range(CHUNK // (2 * n)):
            left, right = 2 * n * p, 2 * n * p + n
            ref = brow(right - 1)
            q_slabs.append(_pad_rows(q_side(ref, right, right + n), right))
            k_slabs.append(_pad_rows(k_side(ref, left, right), left))
        n *= 2
    a = _dot_nt(jnp.concatenate(q_slabs, axis=1).astype(BF16),
                jnp.concatenate(k_slabs, axis=1).astype(BF16))
    q1, k1 = [], []
    for p in range(CHUNK // (2 * SUB)):
        left, right = 2 * SUB * p, 2 * SUB * p + SUB
        ref = brow(right - 1)
        zeros = jnp.zeros((SUB, HEAD_DK), F32)
        q1 += [zeros, q_side(ref, right, right + SUB)]
        k1 += [k_side(ref, left, right), zeros]
    a1 = _dot_nt(jnp.concatenate(q1, axis=0).astype(BF16), jnp.concatenate(k1, axis=0).astype(BF16))
    a = jnp.where(m_pair, a1, a)
    return o, a


def _head_norm_gate(o, gain, gate):
    ms = jnp.mean(o * o, axis=-1, keepdims=True)
    return o * lax.rsqrt(ms + EPS) * gain * (gate * jax.nn.sigmoid(gate))


def _silu(x):
    return x * jax.nn.sigmoid(x)


def _mixer_kernel(z_ref, ga_ref, lb_ref, ghg_ref, convw_ref, wgate_ref, bgate_ref, ggla_ref, rowops_ref,
                  emat_ref, y_ref, sth_ref, stg_ref, sthb_ref, stgb_ref, xh_ref, c_ref, lk_ref, gq_ref, gk_ref,
                  gv_ref, b_ref, o_ref, xd_ref, sd_ref):
    tile = y_ref.shape[0]

    @pl.when(pl.program_id(0) == 0)
    def _():
        sth_ref[...] = jnp.zeros_like(sth_ref)
        stg_ref[...] = jnp.zeros_like(stg_ref)
        sthb_ref[...] = jnp.zeros_like(sthb_ref)
        stgb_ref[...] = jnp.zeros_like(stgb_ref)
        xh_ref[pl.ds(0, HALO), :] = jnp.zeros((HALO, CONV_CH), F32)
        b_ref[pl.ds(0, HALO), :] = jnp.zeros((HALO, DECAY_W), F32)

    lb = lb_ref[...]
    l2_lb = jnp.log(lb) * LOG2E
    l2_1m = jnp.log1p(-lb) * LOG2E
    h2 = z_ref[:, OFF_HF:OFF_HF + HG_WIDTH].astype(F32) * LOG2E
    rhs = l2_1m + _log2_sigmoid(h2)
    log2_f = jnp.maximum(l2_lb, rhs) + _log2_1p_exp2_neg_abs(l2_lb - rhs)
    lk = rhs - h2
    lk_ref[...] = lk

    g2 = (_dot(ga_ref[...].astype(BF16), wgate_ref[...]) + bgate_ref[...]) * LOG2E
    log2_alpha = _log2_sigmoid(g2) * (1.0 / GLA_GATE_NORM)

    tri = rowops_ref[0]
    logd = jnp.concatenate([log2_f, log2_alpha], axis=1)
    hi = logd.astype(BF16)
    lo = (logd - hi.astype(F32)).astype(BF16)
    bcum = _dot(tri, hi) + _dot(tri, lo)
    b_ref[pl.ds(HALO, tile), :] = bcum
    c_ref[...] = bcum[:, :HG_WIDTH] - lk

    x_bf = z_ref[:, OFF_CONV:OFF_CONV + CONV_CH]
    conv = convw_ref[CONV_WIDTH - 1:CONV_WIDTH, :] * x_bf.astype(F32)
    for j in range(CONV_WIDTH - 1):
        conv = conv + convw_ref[j:j + 1, :] * _dot(rowops_ref[CONV_WIDTH - 1 - j], x_bf)
    xh_ref[pl.ds(HALO, HALO), :] = x_bf[:HALO].astype(F32)
    head = convw_ref[0:1, :] * xh_ref[pl.ds(HALO - CONV_WIDTH + 1, HALO), :]
    for j in range(1, CONV_WIDTH):
        head = head + convw_ref[j:j + 1, :] * xh_ref[pl.ds(HALO - CONV_WIDTH + 1 + j, HALO), :]
    xh_ref[pl.ds(0, HALO), :] = x_bf[tile - HALO:].astype(F32)

    def put_qkv(rows, act):
        gq_ref[rows, :] = act[:, :GLA_KEY_WIDTH] * (HEAD_DK ** -0.5)
        gk_ref[rows, :] = act[:, GLA_KEY_WIDTH:2 * GLA_KEY_WIDTH]
        gv_ref[rows, :] = act[:, 2 * GLA_KEY_WIDTH:].astype(BF16)

    put_qkv(pl.ds(0, tile), _silu(conv))
    put_qkv(pl.ds(0, HALO), _silu(head))

    e_mat = emat_ref[...]
    tt = lax.broadcasted_iota(jnp.int32, (CHUNK, CHUNK), 0)
    ss = lax.broadcasted_iota(jnp.int32, (CHUNK, CHUNK), 1)
    m_diag = (jnp.right_shift(tt, LOG2_SUB) == jnp.right_shift(ss, LOG2_SUB)) & (tt >= ss)
    m_pair = (jnp.right_shift(tt, LOG2_SUB + 1) == jnp.right_shift(ss, LOG2_SUB + 1)) & (tt >= ss)

    n_heads = HG_HEADS + GLA_HEADS
    group = 4
    lag = 3

    def chunk_heads(r0):
        rows = pl.ds(r0, CHUNK)
        b_view = b_ref.at[pl.ds(r0, HALO + CHUNK)]
        c_view, lk_view, gk_view = c_ref.at[rows], lk_ref.at[rows], gk_ref.at[rows]
        heads = []
        for h in range(HG_HEADS):
            cs = slice(h * HEAD_DK, (h + 1) * HEAD_DK)
            q = z_ref[rows, OFF_HQ + h * HEAD_DK:OFF_HQ + (h + 1) * HEAD_DK].astype(F32)
            v_bf = z_ref[rows, OFF_HI + h * HG_DV:OFF_HI + (h + 1) * HG_DV]
            heads.append((q, v_bf, cs, ("log", c_view, lk_view, cs), sth_ref, sthb_ref, h,
                          slice(h * HG_DV, (h + 1) * HG_DV)))
        for h in range(GLA_HEADS):
            ks = slice(h * HEAD_DK, (h + 1) * HEAD_DK)
            vs = slice(h * GLA_DV, (h + 1) * GLA_DV)
            bs = slice(HG_WIDTH + h * HEAD_DK, HG_WIDTH + (h + 1) * HEAD_DK)
            heads.append((gq_ref[rows, ks], gv_ref[rows, vs], bs, ("lin", gk_view, ks), stg_ref, stgb_ref,
                          h, slice(HG_WIDTH + h * GLA_DV, HG_WIDTH + (h + 1) * GLA_DV)))
        return rows, b_view, heads

    def diag_group(chunk, g, slot):
        _, b_view, heads = chunk
        for i in range(g * group, (g + 1) * group):
            q, _, bcol, keys = heads[i][:4]
            xd_ref[pl.ds(i * CHUNK, CHUNK), :] = _diag_terms(q, b_view, bcol, keys)
        grows = pl.ds(g * group * CHUNK, group * CHUNK)
        sd_ref[slot, grows, :] = _dot(xd_ref[grows, :], e_mat)

    def rest_group(chunk, g, slot, pending):
        rows, b_view, heads = chunk
        for i in range(g * group, (g + 1) * group):
            q, v_bf, bcol, keys, st_ref, stb_ref, hidx, ocol = heads[i]
            o, a = _head_chunk(q, v_bf, b_view, bcol, keys, st_ref, stb_ref, hidx, m_pair)
            pending = pending + [(rows, slot, i, o, a, v_bf, ocol)]
            if len(pending) > lag:
                finish(*pending[0])
                pending = pending[1:]
        return pending

    def finish(rows, slot, i, o, a, v_bf, ocol):
        sd = sd_ref[slot, pl.ds(i * CHUNK, CHUNK), :]
        a = jnp.where(m_diag, sd[:, :CHUNK], a)
        o_ref[rows, ocol] = o + _dot(a.astype(BF16), v_bf)

    n_chunks = tile // CHUNK
    first = chunk_heads(0)
    for g in range(n_heads // group):
        diag_group(first, g, 0)

    def chunk_body(c, carry):
        cur = chunk_heads(pl.multiple_of(c * CHUNK, CHUNK))
        nxt = chunk_heads(pl.multiple_of((c + 1) * CHUNK, CHUNK))
        slot = c & 1
        pending = []
        for g in range(n_heads // group):
            diag_group(nxt, g, 1 - slot)
            pending = rest_group(cur, g, slot, pending)
        for item in pending:
            finish(*item)
        return carry

    lax.fori_loop(0, n_chunks - 1, chunk_body, 0)
    last = chunk_heads((n_chunks - 1) * CHUNK)
    pending = []
    for g in range(n_heads // group):
        pending = rest_group(last, g, (n_chunks - 1) & 1, pending)
    for item in pending:
        finish(*item)

    for h in range(HG_HEADS):
        cs = slice(h * HG_DV, (h + 1) * HG_DV)
        gate = z_ref[:, OFF_HG + h * HG_DV:OFF_HG + (h + 1) * HG_DV].astype(F32)
        y_ref[:, cs] = _head_norm_gate(o_ref[:, cs], ghg_ref[:, cs], gate).astype(y_ref.dtype)
    for h in range(GLA_HEADS):
        vs = slice(h * GLA_DV, (h + 1) * GLA_DV)
        gate = z_ref[:, OFF_GR + h * GLA_DV:OFF_GR + (h + 1) * GLA_DV].astype(F32)
        y_ref[:, HG_WIDTH + h * GLA_DV:HG_WIDTH + (h + 1) * GLA_DV] = _head_norm_gate(
            o_ref[:, HG_WIDTH + h * GLA_DV:HG_WIDTH + (h + 1) * GLA_DV], ggla_ref[:, vs],
            gate).astype(y_ref.dtype)


def _mixer(z, ga, lb, g_hg, conv_w, w_gate, b_gate, g_gla, tile=MIX_TILE):
    s = z.shape[0]
    const = lambda i: (0, 0)
    lag = jnp.arange(tile)[:, None] - jnp.arange(tile)[None, :]
    rowops = jnp.stack([lag >= 0] + [lag == d for d in range(1, CONV_WIDTH)]).astype(BF16)
    e_mat = ((jnp.arange(SUB * HEAD_DK)[:, None] // HEAD_DK) == (jnp.arange(LANES)[None, :] % SUB)).astype(BF16)
    return pl.pallas_call(
        _mixer_kernel,
        grid=(s // tile,),
        in_specs=[
            pl.BlockSpec((tile, IN_MAIN), lambda i: (i, 0)),
            pl.BlockSpec((tile, LANES), lambda i: (i, 0)),
            pl.BlockSpec((1, HG_WIDTH), const),
            pl.BlockSpec((1, HG_WIDTH), const),
            pl.BlockSpec((CONV_WIDTH, CONV_CH), const),
            pl.BlockSpec((LANES, GLA_KEY_WIDTH), const),
            pl.BlockSpec((1, GLA_KEY_WIDTH), const),
            pl.BlockSpec((1, GLA_WIDTH), const),
            pl.BlockSpec((CONV_WIDTH, tile, tile), lambda i: (0, 0, 0)),
            pl.BlockSpec((SUB * HEAD_DK, LANES), const),
        ],
        out_specs=pl.BlockSpec((tile, D_MODEL), lambda i: (i, 0)),
        out_shape=jax.ShapeDtypeStruct((s, D_MODEL), BF16),
        scratch_shapes=[
            pltpu.VMEM((HG_HEADS, HG_DV, HEAD_DK), F32),
            pltpu.VMEM((GLA_HEADS, GLA_DV, HEAD_DK), F32),
            pltpu.VMEM((HG_HEADS, HG_DV, HEAD_DK), BF16),
            pltpu.VMEM((GLA_HEADS, GLA_DV, HEAD_DK), BF16),
            pltpu.VMEM((2 * HALO, CONV_CH), F32),
            pltpu.VMEM((tile, HG_WIDTH), F32),
            pltpu.VMEM((tile, HG_WIDTH), F32),
            pltpu.VMEM((tile, GLA_KEY_WIDTH), F32),
            pltpu.VMEM((tile, GLA_KEY_WIDTH), F32),
            pltpu.VMEM((tile, GLA_WIDTH), BF16),
            pltpu.VMEM((HALO + tile, DECAY_W), F32),
            pltpu.VMEM((tile, D_MODEL), F32),
            pltpu.VMEM(((HG_HEADS + GLA_HEADS) * CHUNK, SUB * HEAD_DK), BF16),
            pltpu.VMEM((2, (HG_HEADS + GLA_HEADS) * CHUNK, LANES), F32),
        ],
        compiler_params=_cparams(("arbitrary",)),
        name="mixer",
    )(z, ga, lb, g_hg, conv_w, w_gate, b_gate, g_gla, rowops, e_mat)


def _outproj_kernel(y_ref, w_ref, h_ref, o_ref):
    o_ref[...] = h_ref[...] + _dot(y_ref[...], w_ref[...])


def _outproj(y, w, layer, h, tm=512):
    s = h.shape[0]
    row = lambda i: (i, 0)
    return pl.pallas_call(
        _outproj_kernel,
        grid=(s // tm,),
        in_specs=[
            pl.BlockSpec((tm, D_MODEL), row),
            pl.BlockSpec((None, D_MODEL, D_MODEL), lambda i: (layer, 0, 0)),
            pl.BlockSpec((tm, D_MODEL), row),
        ],
        out_specs=pl.BlockSpec((tm, D_MODEL), row),
        out_shape=jax.ShapeDtypeStruct((s, D_MODEL), F32),
        compiler_params=_cparams(("parallel",)),
        name="outproj",
    )(y, w, h)


def _mlp_kernel(h_ref, g_ref, wup_ref, wdown_ref, o_ref, u_ref):
    @pl.when(pl.program_id(1) == 0)
    def _():
        x = h_ref[...]
        u_ref[...] = _rms_scale(x, g_ref[...]).astype(BF16)
        o_ref[...] = x

    m = jnp.maximum(_dot(u_ref[...], wup_ref[...]), 0.0)
    o_ref[...] += _dot((m * m).astype(BF16), wdown_ref[...])


def _mlp(h, g, w_up, w_down, layer, tm=512, tf=1024):
    s = h.shape[0]
    return pl.pallas_call(
        _mlp_kernel,
        grid=(s // tm, D_FF // tf),
        in_specs=[
            pl.BlockSpec((tm, D_MODEL), lambda i, f: (i, 0)),
            pl.BlockSpec((1, D_MODEL), lambda i, f: (0, 0)),
            pl.BlockSpec((None, D_MODEL, tf), lambda i, f: (layer, 0, f)),
            pl.BlockSpec((None, tf, D_MODEL), lambda i, f: (layer, f, 0)),
        ],
        out_specs=pl.BlockSpec((tm, D_MODEL), lambda i, f: (i, 0)),
        out_shape=jax.ShapeDtypeStruct((s, D_MODEL), F32),
        scratch_shapes=[pltpu.VMEM((tm, D_MODEL), BF16)],
        compiler_params=_cparams(("parallel", "arbitrary")),
        name="mlp",
    )(h, g, w_up, w_down)


def _ple_kernel(h_ref, g_ref, wpg_ref, p_ref, wpp_ref, *rest, final):
    x = h_ref[...]
    u = _rms_scale(x, g_ref[...]).astype(BF16)
    gate = jax.nn.sigmoid(_dot(u, wpg_ref[...]))
    y = x + gate * _dot(p_ref[...].astype(BF16), wpp_ref[...])
    if final:
        gf_ref, o_ref = rest
        y = _rms_scale(y, gf_ref[...])
    else:
        (o_ref,) = rest
    o_ref[...] = y


def _ple(h, g, w_pg, p, w_pp, layer, g_final=None, tm=512):
    s = h.shape[0]
    row = lambda i: (i, 0)
    const = lambda i: (0, 0)
    slab = lambda i: (layer, 0, 0)
    in_specs = [
        pl.BlockSpec((tm, D_MODEL), row),
        pl.BlockSpec((1, D_MODEL), const),
        pl.BlockSpec((None, D_MODEL, D_MODEL), slab),
        pl.BlockSpec((None, tm, PLE_DIM), lambda i: (layer, i, 0)),
        pl.BlockSpec((None, PLE_DIM, D_MODEL), slab),
    ]
    args = [h, g, w_pg, p, w_pp]
    if g_final is not None:
        in_specs.append(pl.BlockSpec((1, D_MODEL), const))
        args.append(g_final)
    return pl.pallas_call(
        functools.partial(_ple_kernel, final=g_final is not None),
        grid=(s // tm,),
        in_specs=in_specs,
        out_specs=pl.BlockSpec((tm, D_MODEL), row),
        out_shape=jax.ShapeDtypeStruct((s, D_MODEL), F32),
        compiler_params=_cparams(("parallel",)),
        name="ple",
    )(*args)


def kernel(x, p, g_mix, w_in, lb_logits, g_hg_norm, conv_w, w_gla_gate, b_gla_gate, g_gla_norm,
           w_out, g_mlp, w_up, w_down, g_ple, w_pg, w_pp, g_final):
    batch, seq, _ = x.shape
    depth = w_in.shape[0]
    assert batch == 1 and seq % 1024 == 0
    lb_cum = jnp.cumsum(jax.nn.softmax(lb_logits.astype(F32), axis=0), axis=0)
    lb_all = lb_cum - lb_cum[0:1]
    row = lambda a: a.reshape(1, -1).astype(F32)

    w_in_b = w_in.astype(BF16)
    w_ga = jnp.pad(w_in_b[:, :, IN_MAIN:], ((0, 0), (0, 0), (0, LANES - GLA_GATE_RANK)))
    w_gate = jnp.pad(w_gla_gate, ((0, 0), (0, LANES - GLA_GATE_RANK), (0, 0))).astype(BF16)
    w_out_b, w_up_b, w_down_b = w_out.astype(BF16), w_up.astype(BF16), w_down.astype(BF16)
    w_pg_b, w_pp_b = w_pg.astype(BF16), w_pp.astype(BF16)
    p2 = p.reshape(depth, seq, PLE_DIM)

    h = x.reshape(seq, D_MODEL)
    for l in range(depth):
        z, ga = _inproj(h, row(g_mix[l]), w_in_b, l, w_ga[l])
        y = _mixer(z, ga, row(lb_all[l]), row(g_hg_norm[l]), conv_w[l].astype(F32), w_gate[l],
                   row(b_gla_gate[l]), row(g_gla_norm[l]))
        h = _outproj(y, w_out_b, l, h)
        h = _mlp(h, row(g_mlp[l]), w_up_b, w_down_b, l)
        h = _ple(h, row(g_ple[l]), w_pg_b, p2, w_pp_b, l, row(g_final) if l == depth - 1 else None)
    return h.reshape(batch, seq, D_MODEL)
```

```python
import functools

import jax
import jax.numpy as jnp
from jax import lax
from jax.experimental import pallas as pl
from jax.experimental.pallas import tpu as pltpu

F32 = jnp.float32
BF16 = jnp.bfloat16

EPS = 1e-6
LOG2E = 1.4426950408889634
D_MODEL = 2048
D_FF = 4 * D_MODEL
PLE_DIM = 256
HG_WIDTH = 1024
HG_HEADS = 8
HEAD_DK = 128
HG_DV = 128
GLA_HEADS = 4
GLA_DV = 256
GLA_KEY_WIDTH = GLA_HEADS * HEAD_DK
GLA_WIDTH = GLA_HEADS * GLA_DV
GLA_GATE_RANK = 16
GLA_GATE_NORM = 16.0
CONV_WIDTH = 4
CONV_CH = 2 * GLA_KEY_WIDTH + GLA_WIDTH
IN_MAIN = 4 * HG_WIDTH + CONV_CH + GLA_WIDTH
IN_WIDTH = IN_MAIN + GLA_GATE_RANK
LANES = 128
IN_PAD = IN_MAIN + LANES
SUBLANES = 8

OFF_HQ, OFF_HF, OFF_HI, OFF_HG = 0, HG_WIDTH, 2 * HG_WIDTH, 3 * HG_WIDTH
OFF_CONV = 4 * HG_WIDTH
OFF_GR = OFF_CONV + CONV_CH
DECAY_W = HG_WIDTH + GLA_KEY_WIDTH

CHUNK = 64
SUB = 8
LOG2_SUB = SUB.bit_length() - 1
MIX_TILE = 256
HALO = SUBLANES

VMEM_LIMIT = 56 * 1024 * 1024


def _cparams(sem):
    return pltpu.CompilerParams(dimension_semantics=sem, vmem_limit_bytes=VMEM_LIMIT)


def _rms_scale(x, g):
    ms = jnp.mean(x * x, axis=-1, keepdims=True)
    return x * lax.rsqrt(ms + EPS) * g


def _dot(a, b):
    return jnp.dot(a, b, preferred_element_type=F32)


def _dot_nt(a, b):
    return lax.dot_general(a, b, (((1,), (1,)), ((), ())), preferred_element_type=F32)


def _dot_tn(a, b):
    return lax.dot_general(a, b, (((0,), (0,)), ((), ())), preferred_element_type=F32)


def _inproj_kernel(h_ref, g_ref, w_ref, z_ref, u_ref):
    @pl.when(pl.program_id(1) == 0)
    def _():
        u_ref[...] = _rms_scale(h_ref[...], g_ref[...]).astype(BF16)

    z_ref[...] = _dot(u_ref[...], w_ref[...]).astype(z_ref.dtype)


def _inproj(h, g, w_in, layer, tm=1024, tn=IN_PAD // 3):
    s = h.shape[0]
    assert IN_PAD % tn == 0 and tn % LANES == 0
    return pl.pallas_call(
        _inproj_kernel,
        grid=(s // tm, IN_PAD // tn),
        in_specs=[
            pl.BlockSpec((tm, D_MODEL), lambda i, j: (i, 0)),
            pl.BlockSpec((1, D_MODEL), lambda i, j: (0, 0)),
            pl.BlockSpec((None, D_MODEL, tn), lambda i, j: (layer, 0, j)),
        ],
        out_specs=pl.BlockSpec((tm, tn), lambda i, j: (i, j)),
        out_shape=jax.ShapeDtypeStruct((s, IN_PAD), BF16),
        scratch_shapes=[pltpu.VMEM((tm, D_MODEL), BF16)],
        compiler_params=_cparams(("parallel", "arbitrary")),
        name="inproj",
    )(h, g, w_in)


def _log2_1p_exp2_neg_abs(x):
    return jnp.log(1.0 + jnp.exp2(-jnp.abs(x))) * LOG2E


def _log2_sigmoid(x2):
    return jnp.minimum(x2, 0.0) - _log2_1p_exp2_neg_abs(x2)


def _pad_rows(x, start):
    parts = []
    if start:
        parts.append(jnp.zeros((start, x.shape[1]), x.dtype))
    parts.append(x)
    if start + x.shape[0] < CHUNK:
        parts.append(jnp.zeros((CHUNK - start - x.shape[0], x.shape[1]), x.dtype))
    return jnp.concatenate(parts, axis=0) if len(parts) > 1 else x


def _keys_log(keys):
    return keys[0] == "log"


def _diag_terms(q, b_ref, bcol, keys):
    kcol = keys[-1]
    bv = b_ref[pl.ds(HALO, CHUNK), bcol]
    slabs = []
    for d in range(CHUNK // SUB):
        qb = q[d * SUB:(d + 1) * SUB]
        bb = bv[d * SUB:(d + 1) * SUB]
        xs = []
        for j in range(d * SUB, (d + 1) * SUB):
            if _keys_log(keys):
                e = jnp.exp2(jnp.minimum(bb - keys[1][pl.ds(j, 1), kcol],
                                         keys[2][pl.ds(j, 1), kcol]))
            else:
                e = (jnp.exp2(jnp.minimum(bb - b_ref[pl.ds(HALO + j, 1), bcol], 0.0))
                     * keys[1][pl.ds(j, 1), kcol])
            xs.append(qb * e)
        slabs.append(jnp.concatenate(xs, axis=1))
    return jnp.concatenate(slabs, axis=0).astype(BF16)


def _head_chunk(q, v_bf, b_ref, bcol, keys, st_ref, stb_ref, hidx, m_pair):
    log_keys = _keys_log(keys)
    kcol = keys[-1]

    def brow(j):
        return b_ref[pl.ds(HALO + j, 1), bcol]

    bv = b_ref[pl.ds(HALO, CHUNK), bcol]
    b_prev = brow(-1)
    b_last = brow(CHUNK - 1)
    if log_keys:
        lkv = keys[2][:, kcol]
    else:
        kv = keys[1][:, kcol]

    def q_side(r, lo=0, hi=CHUNK):
        return q[lo:hi] * jnp.exp2(bv[lo:hi] - r)

    def k_side(r, lo=0, hi=CHUNK):
        if log_keys:
            return jnp.exp2(r - bv[lo:hi] + lkv[lo:hi])
        return kv[lo:hi] * jnp.exp2(r - bv[lo:hi])

    o = _dot_nt(q_side(b_prev).astype(BF16), stb_ref[hidx])
    st = st_ref[hidx] * jnp.exp2(b_last - b_prev) + _dot_tn(v_bf, k_side(b_last).astype(BF16))
    st_ref[hidx] = st
    stb_ref[hidx] = st.astype(BF16)

    q_slabs, k_slabs = [], []
    n = 2 * SUB
    while n < CHUNK:
        for p in range(CHUNK // (2 * n)):
            left, right = 2 * n * p, 2 * n * p + n
            ref = brow(right - 1)
            q_slabs.append(_pad_rows(q_side(ref, right, right + n), right))
            k_slabs.append(_pad_rows(k_side(ref, left, right), left))
        n *= 2
    a = _dot_nt(jnp.concatenate(q_slabs, axis=1).astype(BF16),
                jnp.concatenate(k_slabs, axis=1).astype(BF16))
    q1, k1 = [], []
    for p in range(CHUNK // (2 * SUB)):
        left, right = 2 * SUB * p, 2 * SUB * p + SUB
        ref = brow(right - 1)
        zeros = jnp.zeros((SUB, HEAD_DK), F32)
        q1 += [zeros, q_side(ref, right, right + SUB)]
        k1 += [k_side(ref, left, right), zeros]
    a1 = _dot_nt(jnp.concatenate(q1, axis=0).astype(BF16), jnp.concatenate(k1, axis=0).astype(BF16))
    a = jnp.where(m_pair, a1, a)
    return o, a


def _head_norm_gate(o, gain, gate):
    ms = jnp.mean(o * o, axis=-1, keepdims=True)
    return o * lax.rsqrt(ms + EPS) * gain * (gate * jax.nn.sigmoid(gate))


def _silu(x):
    return x * jax.nn.sigmoid(x)


def _mixer_kernel(z_ref, lb_ref, ghg_ref, convw_ref, wgate_ref, bgate_ref, ggla_ref, rowops_ref,
                  emat_ref, y_ref, sth_ref, stg_ref, sthb_ref, stgb_ref, xh_ref, c_ref, lk_ref, gq_ref, gk_ref,
                  gv_ref, b_ref, o_ref, xd_ref, sd_ref):
    tile = y_ref.shape[0]

    @pl.when(pl.program_id(0) == 0)
    def _():
        sth_ref[...] = jnp.zeros_like(sth_ref)
        stg_ref[...] = jnp.zeros_like(stg_ref)
        sthb_ref[...] = jnp.zeros_like(sthb_ref)
        stgb_ref[...] = jnp.zeros_like(stgb_ref)
        xh_ref[pl.ds(0, HALO), :] = jnp.zeros((HALO, CONV_CH), F32)
        b_ref[pl.ds(0, HALO), :] = jnp.zeros((HALO, DECAY_W), F32)

    lb = lb_ref[...]
    l2_lb = jnp.log(lb) * LOG2E
    l2_1m = jnp.log1p(-lb) * LOG2E
    h2 = z_ref[:, OFF_HF:OFF_HF + HG_WIDTH].astype(F32) * LOG2E
    rhs = l2_1m + _log2_sigmoid(h2)
    log2_f = jnp.maximum(l2_lb, rhs) + _log2_1p_exp2_neg_abs(l2_lb - rhs)
    lk = rhs - h2
    lk_ref[...] = lk

    g2 = (_dot(z_ref[:, IN_MAIN:IN_PAD], wgate_ref[...]) + bgate_ref[...]) * LOG2E
    log2_alpha = _log2_sigmoid(g2) * (1.0 / GLA_GATE_NORM)

    tri = rowops_ref[0]
    logd = jnp.concatenate([log2_f, log2_alpha], axis=1)
    hi = logd.astype(BF16)
    lo = (logd - hi.astype(F32)).astype(BF16)
    bcum = _dot(tri, hi) + _dot(tri, lo)
    b_ref[pl.ds(HALO, tile), :] = bcum
    c_ref[...] = bcum[:, :HG_WIDTH] - lk

    x_bf = z_ref[:, OFF_CONV:OFF_CONV + CONV_CH]
    conv = convw_ref[CONV_WIDTH - 1:CONV_WIDTH, :] * x_bf.astype(F32)
    for j in range(CONV_WIDTH - 1):
        conv = conv + convw_ref[j:j + 1, :] * _dot(rowops_ref[CONV_WIDTH - 1 - j], x_bf)
    xh_ref[pl.ds(HALO, HALO), :] = x_bf[:HALO].astype(F32)
    head = convw_ref[0:1, :] * xh_ref[pl.ds(HALO - CONV_WIDTH + 1, HALO), :]
    for j in range(1, CONV_WIDTH):
        head = head + convw_ref[j:j + 1, :] * xh_ref[pl.ds(HALO - CONV_WIDTH + 1 + j, HALO), :]
    xh_ref[pl.ds(0, HALO), :] = x_bf[tile - HALO:].astype(F32)

    def put_qkv(rows, act):
        gq_ref[rows, :] = act[:, :GLA_KEY_WIDTH] * (HEAD_DK ** -0.5)
        gk_ref[rows, :] = act[:, GLA_KEY_WIDTH:2 * GLA_KEY_WIDTH]
        gv_ref[rows, :] = act[:, 2 * GLA_KEY_WIDTH:].astype(BF16)

    put_qkv(pl.ds(0, tile), _silu(conv))
    put_qkv(pl.ds(0, HALO), _silu(head))

    e_mat = emat_ref[...]
    tt = lax.broadcasted_iota(jnp.int32, (CHUNK, CHUNK), 0)
    ss = lax.broadcasted_iota(jnp.int32, (CHUNK, CHUNK), 1)
    m_diag = (jnp.right_shift(tt, LOG2_SUB) == jnp.right_shift(ss, LOG2_SUB)) & (tt >= ss)
    m_pair = (jnp.right_shift(tt, LOG2_SUB + 1) == jnp.right_shift(ss, LOG2_SUB + 1)) & (tt >= ss)

    n_heads = HG_HEADS + GLA_HEADS
    group = 4
    lag = 3

    def chunk_heads(r0):
        rows = pl.ds(r0, CHUNK)
        b_view = b_ref.at[pl.ds(r0, HALO + CHUNK)]
        c_view, lk_view, gk_view = c_ref.at[rows], lk_ref.at[rows], gk_ref.at[rows]
        heads = []
        for h in range(HG_HEADS):
            cs = slice(h * HEAD_DK, (h + 1) * HEAD_DK)
            q = z_ref[rows, OFF_HQ + h * HEAD_DK:OFF_HQ + (h + 1) * HEAD_DK].astype(F32)
            v_bf = z_ref[rows, OFF_HI + h * HG_DV:OFF_HI + (h + 1) * HG_DV]
            heads.append((q, v_bf, cs, ("log", c_view, lk_view, cs), sth_ref, sthb_ref, h,
                          slice(h * HG_DV, (h + 1) * HG_DV)))
        for h in range(GLA_HEADS):
            ks = slice(h * HEAD_DK, (h + 1) * HEAD_DK)
            vs = slice(h * GLA_DV, (h + 1) * GLA_DV)
            bs = slice(HG_WIDTH + h * HEAD_DK, HG_WIDTH + (h + 1) * HEAD_DK)
            heads.append((gq_ref[rows, ks], gv_ref[rows, vs], bs, ("lin", gk_view, ks), stg_ref, stgb_ref,
                          h, slice(HG_WIDTH + h * GLA_DV, HG_WIDTH + (h + 1) * GLA_DV)))
        return rows, b_view, heads

    def diag_group(chunk, g, slot):
        _, b_view, heads = chunk
        for i in range(g * group, (g + 1) * group):
            q, _, bcol, keys = heads[i][:4]
            xd_ref[pl.ds(i * CHUNK, CHUNK), :] = _diag_terms(q, b_view, bcol, keys)
        grows = pl.ds(g * group * CHUNK, group * CHUNK)
        sd_ref[slot, grows, :] = _dot(xd_ref[grows, :], e_mat)

    def rest_group(chunk, g, slot, pending):
        rows, b_view, heads = chunk
        for i in range(g * group, (g + 1) * group):
            q, v_bf, bcol, keys, st_ref, stb_ref, hidx, ocol = heads[i]
            o, a = _head_chunk(q, v_bf, b_view, bcol, keys, st_ref, stb_ref, hidx, m_pair)
            pending = pending + [(rows, slot, i, o, a, v_bf, ocol)]
            if len(pending) > lag:
                finish(*pending[0])
                pending = pending[1:]
        return pending

    def finish(rows, slot, i, o, a, v_bf, ocol):
        sd = sd_ref[slot, pl.ds(i * CHUNK, CHUNK), :]
        a = jnp.where(m_diag, sd[:, :CHUNK], a)
        o_ref[rows, ocol] = o + _dot(a.astype(BF16), v_bf)

    n_chunks = tile // CHUNK
    first = chunk_heads(0)
    for g in range(n_heads // group):
        diag_group(first, g, 0)

    def chunk_body(c, carry):
        cur = chunk_heads(pl.multiple_of(c * CHUNK, CHUNK))
        nxt = chunk_heads(pl.multiple_of((c + 1) * CHUNK, CHUNK))
        slot = c & 1
        pending = []
        for g in range(n_heads // group):
            diag_group(nxt, g, 1 - slot)
            pending = rest_group(cur, g, slot, pending)
        for item in pending:
            finish(*item)
        return carry

    lax.fori_loop(0, n_chunks - 1, chunk_body, 0)
    last = chunk_heads((n_chunks - 1) * CHUNK)
    pending = []
    for g in range(n_heads // group):
        pending = rest_group(last, g, (n_chunks - 1) & 1, pending)
    for item in pending:
        finish(*item)

    for h in range(HG_HEADS):
        cs = slice(h * HG_DV, (h + 1) * HG_DV)
        gate = z_ref[:, OFF_HG + h * HG_DV:OFF_HG + (h + 1) * HG_DV].astype(F32)
        y_ref[:, cs] = _head_norm_gate(o_ref[:, cs], ghg_ref[:, cs], gate).astype(y_ref.dtype)
    for h in range(GLA_HEADS):
        vs = slice(h * GLA_DV, (h + 1) * GLA_DV)
        gate = z_ref[:, OFF_GR + h * GLA_DV:OFF_GR + (h + 1) * GLA_DV].astype(F32)
        y_ref[:, HG_WIDTH + h * GLA_DV:HG_WIDTH + (h + 1) * GLA_DV] = _head_norm_gate(
            o_ref[:, HG_WIDTH + h * GLA_DV:HG_WIDTH + (h + 1) * GLA_DV], ggla_ref[:, vs],
            gate).astype(y_ref.dtype)


def _mixer(z, lb, g_hg, conv_w, w_gate, b_gate, g_gla, tile=MIX_TILE):
    s = z.shape[0]
    const = lambda i: (0, 0)
    lag = jnp.arange(tile)[:, None] - jnp.arange(tile)[None, :]
    rowops = jnp.stack([lag >= 0] + [lag == d for d in range(1, CONV_WIDTH)]).astype(BF16)
    e_mat = ((jnp.arange(SUB * HEAD_DK)[:, None] // HEAD_DK) == (jnp.arange(LANES)[None, :] % SUB)).astype(BF16)
    return pl.pallas_call(
        _mixer_kernel,
        grid=(s // tile,),
        in_specs=[
            pl.BlockSpec((tile, IN_PAD), lambda i: (i, 0)),
            pl.BlockSpec((1, HG_WIDTH), const),
            pl.BlockSpec((1, HG_WIDTH), const),
            pl.BlockSpec((CONV_WIDTH, CONV_CH), const),
            pl.BlockSpec((LANES, GLA_KEY_WIDTH), const),
            pl.BlockSpec((1, GLA_KEY_WIDTH), const),
            pl.BlockSpec((1, GLA_WIDTH), const),
            pl.BlockSpec((CONV_WIDTH, tile, tile), lambda i: (0, 0, 0)),
            pl.BlockSpec((SUB * HEAD_DK, LANES), const),
        ],
        out_specs=pl.BlockSpec((tile, D_MODEL), lambda i: (i, 0)),
        out_shape=jax.ShapeDtypeStruct((s, D_MODEL), BF16),
        scratch_shapes=[
            pltpu.VMEM((HG_HEADS, HG_DV, HEAD_DK), F32),
            pltpu.VMEM((GLA_HEADS, GLA_DV, HEAD_DK), F32),
            pltpu.VMEM((HG_HEADS, HG_DV, HEAD_DK), BF16),
            pltpu.VMEM((GLA_HEADS, GLA_DV, HEAD_DK), BF16),
            pltpu.VMEM((2 * HALO, CONV_CH), F32),
            pltpu.VMEM((tile, HG_WIDTH), F32),
            pltpu.VMEM((tile, HG_WIDTH), F32),
            pltpu.VMEM((tile, GLA_KEY_WIDTH), F32),
            pltpu.VMEM((tile, GLA_KEY_WIDTH), F32),
            pltpu.VMEM((tile, GLA_WIDTH), BF16),
            pltpu.VMEM((HALO + tile, DECAY_W), F32),
            pltpu.VMEM((tile, D_MODEL), F32),
            pltpu.VMEM(((HG_HEADS + GLA_HEADS) * CHUNK, SUB * HEAD_DK), BF16),
            pltpu.VMEM((2, (HG_HEADS + GLA_HEADS) * CHUNK, LANES), F32),
        ],
        compiler_params=_cparams(("arbitrary",)),
        name="mixer",
    )(z, lb, g_hg, conv_w, w_gate, b_gate, g_gla, rowops, e_mat)


def _outproj_kernel(y_ref, w_ref, h_ref, o_ref):
    o_ref[...] = h_ref[...] + _dot(y_ref[...], w_ref[...])


def _outproj(y, w, layer, h, tm=512):
    s = h.shape[0]
    row = lambda i: (i, 0)
    return pl.pallas_call(
        _outproj_kernel,
        grid=(s // tm,),
        in_specs=[
            pl.BlockSpec((tm, D_MODEL), row),
            pl.BlockSpec((None, D_MODEL, D_MODEL), lambda i: (layer, 0, 0)),
            pl.BlockSpec((tm, D_MODEL), row),
        ],
        out_specs=pl.BlockSpec((tm, D_MODEL), row),
        out_shape=jax.ShapeDtypeStruct((s, D_MODEL), F32),
        compiler_params=_cparams(("parallel",)),
        name="outproj",
    )(y, w, h)


def _mlp_kernel(h_ref, g_ref, wup_ref, wdown_ref, o_ref, u_ref):
    @pl.when(pl.program_id(1) == 0)
    def _():
        x = h_ref[...]
        u_ref[...] = _rms_scale(x, g_ref[...]).astype(BF16)
        o_ref[...] = x

    m = jnp.maximum(_dot(u_ref[...], wup_ref[...]), 0.0)
    o_ref[...] += _dot((m * m).astype(BF16), wdown_ref[...])


def _mlp(h, g, w_up, w_down, layer, tm=512, tf=1024):
    s = h.shape[0]
    return pl.pallas_call(
        _mlp_kernel,
        grid=(s // tm, D_FF // tf),
        in_specs=[
            pl.BlockSpec((tm, D_MODEL), lambda i, f: (i, 0)),
            pl.BlockSpec((1, D_MODEL), lambda i, f: (0, 0)),
            pl.BlockSpec((None, D_MODEL, tf), lambda i, f: (layer, 0, f)),
            pl.BlockSpec((None, tf, D_MODEL), lambda i, f: (layer, f, 0)),
        ],
        out_specs=pl.BlockSpec((tm, D_MODEL), lambda i, f: (i, 0)),
        out_shape=jax.ShapeDtypeStruct((s, D_MODEL), F32),
        scratch_shapes=[pltpu.VMEM((tm, D_MODEL), BF16)],
        compiler_params=_cparams(("parallel", "arbitrary")),
        name="mlp",
    )(h, g, w_up, w_down)


def _ple_kernel(h_ref, g_ref, wpg_ref, p_ref, wpp_ref, *rest, final):
    x = h_ref[...]
    u = _rms_scale(x, g_ref[...]).astype(BF16)
    gate = jax.nn.sigmoid(_dot(u, wpg_ref[...]))
    y = x + gate * _dot(p_ref[...].astype(BF16), wpp_ref[...])
    if final:
        gf_ref, o_ref = rest
        y = _rms_scale(y, gf_ref[...])
    else:
        (o_ref,) = rest
    o_ref[...] = y


def _ple(h, g, w_pg, p, w_pp, layer, g_final=None, tm=512):
    s = h.shape[0]
    row = lambda i: (i, 0)
    const = lambda i: (0, 0)
    slab = lambda i: (layer, 0, 0)
    in_specs = [
        pl.BlockSpec((tm, D_MODEL), row),
        pl.BlockSpec((1, D_MODEL), const),
        pl.BlockSpec((None, D_MODEL, D_MODEL), slab),
        pl.BlockSpec((None, tm, PLE_DIM), lambda i: (layer, i, 0)),
        pl.BlockSpec((None, PLE_DIM, D_MODEL), slab),
    ]
    args = [h, g, w_pg, p, w_pp]
    if g_final is not None:
        in_specs.append(pl.BlockSpec((1, D_MODEL), const))
        args.append(g_final)
    return pl.pallas_call(
        functools.partial(_ple_kernel, final=g_final is not None),
        grid=(s // tm,),
        in_specs=in_specs,
        out_specs=pl.BlockSpec((tm, D_MODEL), row),
        out_shape=jax.ShapeDtypeStruct((s, D_MODEL), F32),
        compiler_params=_cparams(("parallel",)),
        name="ple",
    )(*args)


def kernel(x, p, g_mix, w_in, lb_logits, g_hg_norm, conv_w, w_gla_gate, b_gla_gate, g_gla_norm,
           w_out, g_mlp, w_up, w_down, g_ple, w_pg, w_pp, g_final):
    batch, seq, _ = x.shape
    depth = w_in.shape[0]
    assert batch == 1 and seq % 1024 == 0
    lb_cum = jnp.cumsum(jax.nn.softmax(lb_logits.astype(F32), axis=0), axis=0)
    lb_all = lb_cum - lb_cum[0:1]
    row = lambda a: a.reshape(1, -1).astype(F32)

    w_in_b = jnp.pad(w_in, ((0, 0), (0, 0), (0, IN_PAD - IN_WIDTH))).astype(BF16)
    w_gate = jnp.pad(w_gla_gate, ((0, 0), (0, LANES - GLA_GATE_RANK), (0, 0))).astype(BF16)
    w_out_b, w_up_b, w_down_b = w_out.astype(BF16), w_up.astype(BF16), w_down.astype(BF16)
    w_pg_b, w_pp_b = w_pg.astype(BF16), w_pp.astype(BF16)
    p2 = p.reshape(depth, seq, PLE_DIM)

    h = x.reshape(seq, D_MODEL)
    for l in range(depth):
        z = _inproj(h, row(g_mix[l]), w_in_b, l)
        y = _mixer(z, row(lb_all[l]), row(g_hg_norm[l]), conv_w[l].astype(F32), w_gate[l],
                   row(b_gla_gate[l]), row(g_gla_norm[l]))
        h = _outproj(y, w_out_b, l, h)
        h = _mlp(h, row(g_mlp[l]), w_up_b, w_down_b, l)
        h = _ple(h, row(g_ple[l]), w_pg_b, p2, w_pp_b, l, row(g_final) if l == depth - 1 else None)
    return h.reshape(batch, seq, D_MODEL)
```

```python
import functools

import jax
import jax.numpy as jnp
from jax import lax
from jax.experimental import pallas as pl
from jax.experimental.pallas import tpu as pltpu

F32 = jnp.float32
BF16 = jnp.bfloat16

EPS = 1e-6
LOG2E = 1.4426950408889634
D_MODEL = 2048
D_FF = 4 * D_MODEL
PLE_DIM = 256
HG_WIDTH = 1024
HG_HEADS = 8
HEAD_DK = 128
HG_DV = 128
GLA_HEADS = 4
GLA_DV = 256
GLA_KEY_WIDTH = GLA_HEADS * HEAD_DK
GLA_WIDTH = GLA_HEADS * GLA_DV
GLA_GATE_RANK = 16
GLA_GATE_NORM = 16.0
CONV_WIDTH = 4
CONV_CH = 2 * GLA_KEY_WIDTH + GLA_WIDTH
IN_MAIN = 4 * HG_WIDTH + CONV_CH + GLA_WIDTH
LANES = 128
SUBLANES = 8

OFF_HQ, OFF_HF, OFF_HI, OFF_HG = 0, HG_WIDTH, 2 * HG_WIDTH, 3 * HG_WIDTH
OFF_CONV = 4 * HG_WIDTH
OFF_GR = OFF_CONV + CONV_CH
DECAY_W = HG_WIDTH + GLA_KEY_WIDTH

CHUNK = 64
SUB = 8
LOG2_SUB = SUB.bit_length() - 1
MIX_TILE = 256
HALO = SUBLANES

VMEM_LIMIT = 56 * 1024 * 1024


def _cparams(sem):
    return pltpu.CompilerParams(dimension_semantics=sem, vmem_limit_bytes=VMEM_LIMIT)


def _rms_scale(x, g):
    ms = jnp.mean(x * x, axis=-1, keepdims=True)
    return x * lax.rsqrt(ms + EPS) * g


def _dot(a, b):
    return jnp.dot(a, b, preferred_element_type=F32)


def _dot_nt(a, b):
    return lax.dot_general(a, b, (((1,), (1,)), ((), ())), preferred_element_type=F32)


def _dot_tn(a, b):
    return lax.dot_general(a, b, (((0,), (0,)), ((), ())), preferred_element_type=F32)


def _inproj_kernel(h_ref, g_ref, w_ref, wga_ref, z_ref, ga_ref, u_ref):
    @pl.when(pl.program_id(1) == 0)
    def _():
        u_ref[...] = _rms_scale(h_ref[...], g_ref[...]).astype(BF16)
        ga_ref[...] = _dot(u_ref[...], wga_ref[...])

    z_ref[...] = _dot(u_ref[...], w_ref[...]).astype(z_ref.dtype)


def _inproj(h, g, w_in, layer, w_ga, tm=1024, tn=1792):
    s = h.shape[0]
    return pl.pallas_call(
        _inproj_kernel,
        grid=(s // tm, IN_MAIN // tn),
        in_specs=[
            pl.BlockSpec((tm, D_MODEL), lambda i, j: (i, 0)),
            pl.BlockSpec((1, D_MODEL), lambda i, j: (0, 0)),
            pl.BlockSpec((None, D_MODEL, tn), lambda i, j: (layer, 0, j)),
            pl.BlockSpec((D_MODEL, LANES), lambda i, j: (0, 0)),
        ],
        out_specs=[
            pl.BlockSpec((tm, tn), lambda i, j: (i, j)),
            pl.BlockSpec((tm, LANES), lambda i, j: (i, 0)),
        ],
        out_shape=[
            jax.ShapeDtypeStruct((s, IN_MAIN), BF16),
            jax.ShapeDtypeStruct((s, LANES), F32),
        ],
        scratch_shapes=[pltpu.VMEM((tm, D_MODEL), BF16)],
        compiler_params=_cparams(("parallel", "arbitrary")),
        name="inproj",
    )(h, g, w_in, w_ga)


def _log2_1p_exp2_neg_abs(x):
    return jnp.log(1.0 + jnp.exp2(-jnp.abs(x))) * LOG2E


def _log2_sigmoid(x2):
    return jnp.minimum(x2, 0.0) - _log2_1p_exp2_neg_abs(x2)


def _pad_rows(x, start):
    parts = []
    if start:
        parts.append(jnp.zeros((start, x.shape[1]), x.dtype))
    parts.append(x)
    if start + x.shape[0] < CHUNK:
        parts.append(jnp.zeros((CHUNK - start - x.shape[0], x.shape[1]), x.dtype))
    return jnp.concatenate(parts, axis=0) if len(parts) > 1 else x


def _keys_log(keys):
    return keys[0] == "log"


def _diag_terms(q, b_ref, bcol, keys):
    kcol = keys[-1]
    bv = b_ref[pl.ds(HALO, CHUNK), bcol]
    slabs = []
    for d in range(CHUNK // SUB):
        qb = q[d * SUB:(d + 1) * SUB]
        bb = bv[d * SUB:(d + 1) * SUB]
        xs = []
        for j in range(d * SUB, (d + 1) * SUB):
            if _keys_log(keys):
                e = jnp.exp2(jnp.minimum(bb - keys[1][pl.ds(j, 1), kcol],
                                         keys[2][pl.ds(j, 1), kcol]))
            else:
                e = (jnp.exp2(jnp.minimum(bb - b_ref[pl.ds(HALO + j, 1), bcol], 0.0))
                     * keys[1][pl.ds(j, 1), kcol])
            xs.append(qb * e)
        slabs.append(jnp.concatenate(xs, axis=1))
    return jnp.concatenate(slabs, axis=0).astype(BF16)


def _head_chunk(q, v_bf, b_ref, bcol, keys, st_ref, stb_ref, hidx, m_pair):
    log_keys = _keys_log(keys)
    kcol = keys[-1]

    def brow(j):
        return b_ref[pl.ds(HALO + j, 1), bcol]

    bv = b_ref[pl.ds(HALO, CHUNK), bcol]
    b_prev = brow(-1)
    b_last = brow(CHUNK - 1)
    if log_keys:
        lkv = keys[2][:, kcol]
    else:
        kv = keys[1][:, kcol]

    def q_side(r, lo=0, hi=CHUNK):
        return q[lo:hi] * jnp.exp2(bv[lo:hi] - r)

    def k_side(r, lo=0, hi=CHUNK):
        if log_keys:
            return jnp.exp2(r - bv[lo:hi] + lkv[lo:hi])
        return kv[lo:hi] * jnp.exp2(r - bv[lo:hi])

    o = _dot_nt(q_side(b_prev).astype(BF16), stb_ref[hidx])
    st = st_ref[hidx] * jnp.exp2(b_last - b_prev) + _dot_tn(v_bf, k_side(b_last).astype(BF16))
    st_ref[hidx] = st
    stb_ref[hidx] = st.astype(BF16)

    q_slabs, k_slabs = [], []
    n = 2 * SUB
    while n < CHUNK:
        for p in range(CHUNK // (2 * n)):
            left, right = 2 * n * p, 2 * n * p + n
            ref = brow(right - 1)
            q_slabs.append(_pad_rows(q_side(ref, right, right + n), right))
            k_slabs.append(_pad_rows(k_side(ref, left, right), left))
        n *= 2
    a = _dot_nt(jnp.concatenate(q_slabs, axis=1).astype(BF16),
                jnp.concatenate(k_slabs, axis=1).astype(BF16))
    q1, k1 = [], []
    for p in range(CHUNK // (2 * SUB)):
        left, right = 2 * SUB * p, 2 * SUB * p + SUB
        ref = brow(right - 1)
        zeros = jnp.zeros((SUB, HEAD_DK), F32)
        q1 += [zeros, q_side(ref, right, right + SUB)]
        k1 += [k_side(ref, left, right), zeros]
    a1 = _dot_nt(jnp.concatenate(q1, axis=0).astype(BF16), jnp.concatenate(k1, axis=0).astype(BF16))
    a = jnp.where(m_pair, a1, a)
    return o, a


def _head_norm_gate(o, gain, gate):
    ms = jnp.mean(o * o, axis=-1, keepdims=True)
    return o * lax.rsqrt(ms + EPS) * gain * (gate * jax.nn.sigmoid(gate))


def _silu(x):
    return x * jax.nn.sigmoid(x)


def _mixer_kernel(z_ref, ga_ref, lb_ref, ghg_ref, convw_ref, wgate_ref, bgate_ref, ggla_ref, rowops_ref,
                  emat_ref, y_ref, sth_ref, stg_ref, sthb_ref, stgb_ref, xh_ref, c_ref, lk_ref, gq_ref, gk_ref,
                  gv_ref, b_ref, xd_ref, sd_ref):
    tile = y_ref.shape[0]

    @pl.when(pl.program_id(0) == 0)
    def _():
        sth_ref[...] = jnp.zeros_like(sth_ref)
        stg_ref[...] = jnp.zeros_like(stg_ref)
        sthb_ref[...] = jnp.zeros_like(sthb_ref)
        stgb_ref[...] = jnp.zeros_like(stgb_ref)
        xh_ref[pl.ds(0, HALO), :] = jnp.zeros((HALO, CONV_CH), F32)
        b_ref[pl.ds(0, HALO), :] = jnp.zeros((HALO, DECAY_W), F32)

    lb = lb_ref[...]
    l2_lb = jnp.log(lb) * LOG2E
    l2_1m = jnp.log1p(-lb) * LOG2E
    h2 = z_ref[:, OFF_HF:OFF_HF + HG_WIDTH].astype(F32) * LOG2E
    rhs = l2_1m + _log2_sigmoid(h2)
    log2_f = jnp.maximum(l2_lb, rhs) + _log2_1p_exp2_neg_abs(l2_lb - rhs)
    lk = rhs - h2
    lk_ref[...] = lk

    g2 = (_dot(ga_ref[...].astype(BF16), wgate_ref[...]) + bgate_ref[...]) * LOG2E
    log2_alpha = _log2_sigmoid(g2) * (1.0 / GLA_GATE_NORM)

    tri = rowops_ref[0]
    logd = jnp.concatenate([log2_f, log2_alpha], axis=1)
    hi = logd.astype(BF16)
    lo = (logd - hi.astype(F32)).astype(BF16)
    bcum = _dot(tri, hi) + _dot(tri, lo)
    b_ref[pl.ds(HALO, tile), :] = bcum
    c_ref[...] = bcum[:, :HG_WIDTH] - lk

    x_bf = z_ref[:, OFF_CONV:OFF_CONV + CONV_CH]
    conv = convw_ref[CONV_WIDTH - 1:CONV_WIDTH, :] * x_bf.astype(F32)
    for j in range(CONV_WIDTH - 1):
        conv = conv + convw_ref[j:j + 1, :] * _dot(rowops_ref[CONV_WIDTH - 1 - j], x_bf)
    xh_ref[pl.ds(HALO, HALO), :] = x_bf[:HALO].astype(F32)
    head = convw_ref[0:1, :] * xh_ref[pl.ds(HALO - CONV_WIDTH + 1, HALO), :]
    for j in range(1, CONV_WIDTH):
        head = head + convw_ref[j:j + 1, :] * xh_ref[pl.ds(HALO - CONV_WIDTH + 1 + j, HALO), :]
    xh_ref[pl.ds(0, HALO), :] = x_bf[tile - HALO:].astype(F32)

    def put_qkv(rows, act):
        gq_ref[rows, :] = act[:, :GLA_KEY_WIDTH] * (HEAD_DK ** -0.5)
        gk_ref[rows, :] = act[:, GLA_KEY_WIDTH:2 * GLA_KEY_WIDTH]
        gv_ref[rows, :] = act[:, 2 * GLA_KEY_WIDTH:].astype(BF16)

    put_qkv(pl.ds(0, tile), _silu(conv))
    put_qkv(pl.ds(0, HALO), _silu(head))

    e_mat = emat_ref[...]
    tt = lax.broadcasted_iota(jnp.int32, (CHUNK, CHUNK), 0)
    ss = lax.broadcasted_iota(jnp.int32, (CHUNK, CHUNK), 1)
    m_diag = (jnp.right_shift(tt, LOG2_SUB) == jnp.right_shift(ss, LOG2_SUB)) & (tt >= ss)
    m_pair = (jnp.right_shift(tt, LOG2_SUB + 1) == jnp.right_shift(ss, LOG2_SUB + 1)) & (tt >= ss)

    n_heads = HG_HEADS + GLA_HEADS
    group = 4
    lag = 3

    def chunk_heads(r0):
        rows = pl.ds(r0, CHUNK)
        b_view = b_ref.at[pl.ds(r0, HALO + CHUNK)]
        c_view, lk_view, gk_view = c_ref.at[rows], lk_ref.at[rows], gk_ref.at[rows]
        heads = []
        for h in range(HG_HEADS):
            cs = slice(h * HEAD_DK, (h + 1) * HEAD_DK)
            q = z_ref[rows, OFF_HQ + h * HEAD_DK:OFF_HQ + (h + 1) * HEAD_DK].astype(F32)
            v_bf = z_ref[rows, OFF_HI + h * HG_DV:OFF_HI + (h + 1) * HG_DV]
            heads.append((q, v_bf, cs, ("log", c_view, lk_view, cs), sth_ref, sthb_ref, h,
                          slice(h * HG_DV, (h + 1) * HG_DV)))
        for h in range(GLA_HEADS):
            ks = slice(h * HEAD_DK, (h + 1) * HEAD_DK)
            vs = slice(h * GLA_DV, (h + 1) * GLA_DV)
            bs = slice(HG_WIDTH + h * HEAD_DK, HG_WIDTH + (h + 1) * HEAD_DK)
            heads.append((gq_ref[rows, ks], gv_ref[rows, vs], bs, ("lin", gk_view, ks), stg_ref, stgb_ref,
                          h, slice(HG_WIDTH + h * GLA_DV, HG_WIDTH + (h + 1) * GLA_DV)))
        return rows, b_view, heads

    def diag_group(chunk, g, slot):
        _, b_view, heads = chunk
        for i in range(g * group, (g + 1) * group):
            q, _, bcol, keys = heads[i][:4]
            xd_ref[pl.ds(i * CHUNK, CHUNK), :] = _diag_terms(q, b_view, bcol, keys)
        grows = pl.ds(g * group * CHUNK, group * CHUNK)
        sd_ref[slot, grows, :] = _dot(xd_ref[grows, :], e_mat)

    def rest_group(chunk, g, slot, pending):
        rows, b_view, heads = chunk
        for i in range(g * group, (g + 1) * group):
            q, v_bf, bcol, keys, st_ref, stb_ref, hidx, ocol = heads[i]
            o, a = _head_chunk(q, v_bf, b_view, bcol, keys, st_ref, stb_ref, hidx, m_pair)
            pending = pending + [(rows, slot, i, o, a, v_bf, ocol)]
            if len(pending) > lag:
                finish(*pending[0])
                pending = pending[1:]
        return pending

    def finish(rows, slot, i, o, a, v_bf, ocol):
        sd = sd_ref[slot, pl.ds(i * CHUNK, CHUNK), :]
        a = jnp.where(m_diag, sd[:, :CHUNK], a)
        y_ref[rows, ocol] = o + _dot(a.astype(BF16), v_bf)

    n_chunks = tile // CHUNK
    first = chunk_heads(0)
    for g in range(n_heads // group):
        diag_group(first, g, 0)

    def chunk_body(c, carry):
        cur = chunk_heads(pl.multiple_of(c * CHUNK, CHUNK))
        nxt = chunk_heads(pl.multiple_of((c + 1) * CHUNK, CHUNK))
        slot = c & 1
        pending = []
        for g in range(n_heads // group):
            diag_group(nxt, g, 1 - slot)
            pending = rest_group(cur, g, slot, pending)
        for item in pending:
            finish(*item)
        return carry

    lax.fori_loop(0, n_chunks - 1, chunk_body, 0)
    last = chunk_heads((n_chunks - 1) * CHUNK)
    pending = []
    for g in range(n_heads // group):
        pending = rest_group(last, g, (n_chunks - 1) & 1, pending)
    for item in pending:
        finish(*item)


def _mixer(z, ga, lb, g_hg, conv_w, w_gate, b_gate, g_gla, tile=MIX_TILE):
    s = z.shape[0]
    const = lambda i: (0, 0)
    lag = jnp.arange(tile)[:, None] - jnp.arange(tile)[None, :]
    rowops = jnp.stack([lag >= 0] + [lag == d for d in range(1, CONV_WIDTH)]).astype(BF16)
    e_mat = ((jnp.arange(SUB * HEAD_DK)[:, None] // HEAD_DK) == (jnp.arange(LANES)[None, :] % SUB)).astype(BF16)
    return pl.pallas_call(
        _mixer_kernel,
        grid=(s // tile,),
        in_specs=[
            pl.BlockSpec((tile, IN_MAIN), lambda i: (i, 0)),
            pl.BlockSpec((tile, LANES), lambda i: (i, 0)),
            pl.BlockSpec((1, HG_WIDTH), const),
            pl.BlockSpec((1, HG_WIDTH), const),
            pl.BlockSpec((CONV_WIDTH, CONV_CH), const),
            pl.BlockSpec((LANES, GLA_KEY_WIDTH), const),
            pl.BlockSpec((1, GLA_KEY_WIDTH), const),
            pl.BlockSpec((1, GLA_WIDTH), const),
            pl.BlockSpec((CONV_WIDTH, tile, tile), lambda i: (0, 0, 0)),
            pl.BlockSpec((SUB * HEAD_DK, LANES), const),
        ],
        out_specs=pl.BlockSpec((tile, D_MODEL), lambda i: (i, 0)),
        out_shape=jax.ShapeDtypeStruct((s, D_MODEL), F32),
        scratch_shapes=[
            pltpu.VMEM((HG_HEADS, HG_DV, HEAD_DK), F32),
            pltpu.VMEM((GLA_HEADS, GLA_DV, HEAD_DK), F32),
            pltpu.VMEM((HG_HEADS, HG_DV, HEAD_DK), BF16),
            pltpu.VMEM((GLA_HEADS, GLA_DV, HEAD_DK), BF16),
            pltpu.VMEM((2 * HALO, CONV_CH), F32),
            pltpu.VMEM((tile, HG_WIDTH), F32),
            pltpu.VMEM((tile, HG_WIDTH), F32),
            pltpu.VMEM((tile, GLA_KEY_WIDTH), F32),
            pltpu.VMEM((tile, GLA_KEY_WIDTH), F32),
            pltpu.VMEM((tile, GLA_WIDTH), BF16),
            pltpu.VMEM((HALO + tile, DECAY_W), F32),
            pltpu.VMEM(((HG_HEADS + GLA_HEADS) * CHUNK, SUB * HEAD_DK), BF16),
            pltpu.VMEM((2, (HG_HEADS + GLA_HEADS) * CHUNK, LANES), F32),
        ],
        compiler_params=_cparams(("arbitrary",)),
        name="mixer",
    )(z, ga, lb, g_hg, conv_w, w_gate, b_gate, g_gla, rowops, e_mat)


def _outproj_kernel(o_ref, zhg_ref, zgr_ref, ghg_ref, ggla_ref, w_ref, h_ref, out_ref, y_ref):
    i = pl.program_id(0)
    slot = i & 1

    @pl.when(i == 0)
    def _():
        y_ref[1] = jnp.zeros(y_ref.shape[1:], y_ref.dtype)

    for h in range(HG_HEADS):
        cs = slice(h * HG_DV, (h + 1) * HG_DV)
        y_ref[slot, :, cs] = _head_norm_gate(o_ref[:, cs], ghg_ref[:, cs],
                                             zhg_ref[:, cs].astype(F32)).astype(BF16)
    for h in range(GLA_HEADS):
        vs = slice(h * GLA_DV, (h + 1) * GLA_DV)
        os_ = slice(HG_WIDTH + h * GLA_DV, HG_WIDTH + (h + 1) * GLA_DV)
        y_ref[slot, :, os_] = _head_norm_gate(o_ref[:, os_], ggla_ref[:, vs],
                                              zgr_ref[:, vs].astype(F32)).astype(BF16)
    out_ref[...] = h_ref[...] + _dot(y_ref[1 - slot], w_ref[...])


def _outproj(o, z, g_hg, g_gla, w, layer, h, tm=512):
    s = h.shape[0]
    n = s // tm
    cur = lambda i: (jnp.minimum(i, n - 1), 0)
    prev = lambda i: (jnp.maximum(i - 1, 0), 0)
    const = lambda i: (0, 0)
    return pl.pallas_call(
        _outproj_kernel,
        grid=(n + 1,),
        in_specs=[
            pl.BlockSpec((tm, D_MODEL), cur),
            pl.BlockSpec((tm, HG_WIDTH), lambda i: (jnp.minimum(i, n - 1), OFF_HG // HG_WIDTH)),
            pl.BlockSpec((tm, GLA_WIDTH), lambda i: (jnp.minimum(i, n - 1), OFF_GR // GLA_WIDTH)),
            pl.BlockSpec((1, HG_WIDTH), const),
            pl.BlockSpec((1, GLA_WIDTH), const),
            pl.BlockSpec((None, D_MODEL, D_MODEL), lambda i: (layer, 0, 0)),
            pl.BlockSpec((tm, D_MODEL), prev),
        ],
        out_specs=pl.BlockSpec((tm, D_MODEL), prev),
        out_shape=jax.ShapeDtypeStruct((s, D_MODEL), F32),
        scratch_shapes=[pltpu.VMEM((2, tm, D_MODEL), BF16)],
        compiler_params=_cparams(("arbitrary",)),
        name="outproj",
    )(o, z, z, g_hg, g_gla, w, h)


def _mlp_kernel(h_ref, g_ref, wup_ref, wdown_ref, o_ref, u_ref):
    @pl.when(pl.program_id(1) == 0)
    def _():
        x = h_ref[...]
        u_ref[...] = _rms_scale(x, g_ref[...]).astype(BF16)
        o_ref[...] = x

    m = jnp.maximum(_dot(u_ref[...], wup_ref[...]), 0.0)
    o_ref[...] += _dot((m * m).astype(BF16), wdown_ref[...])


def _mlp(h, g, w_up, w_down, layer, tm=512, tf=1024):
    s = h.shape[0]
    return pl.pallas_call(
        _mlp_kernel,
        grid=(s // tm, D_FF // tf),
        in_specs=[
            pl.BlockSpec((tm, D_MODEL), lambda i, f: (i, 0)),
            pl.BlockSpec((1, D_MODEL), lambda i, f: (0, 0)),
            pl.BlockSpec((None, D_MODEL, tf), lambda i, f: (layer, 0, f)),
            pl.BlockSpec((None, tf, D_MODEL), lambda i, f: (layer, f, 0)),
        ],
        out_specs=pl.BlockSpec((tm, D_MODEL), lambda i, f: (i, 0)),
        out_shape=jax.ShapeDtypeStruct((s, D_MODEL), F32),
        scratch_shapes=[pltpu.VMEM((tm, D_MODEL), BF16)],
        compiler_params=_cparams(("parallel", "arbitrary")),
        name="mlp",
    )(h, g, w_up, w_down)


def _ple_kernel(h_ref, g_ref, wpg_ref, p_ref, wpp_ref, *rest, final):
    x = h_ref[...]
    u = _rms_scale(x, g_ref[...]).astype(BF16)
    gate = jax.nn.sigmoid(_dot(u, wpg_ref[...]))
    y = x + gate * _dot(p_ref[...].astype(BF16), wpp_ref[...])
    if final:
        gf_ref, o_ref = rest
        y = _rms_scale(y, gf_ref[...])
    else:
        (o_ref,) = rest
    o_ref[...] = y


def _ple(h, g, w_pg, p, w_pp, layer, g_final=None, tm=512):
    s = h.shape[0]
    row = lambda i: (i, 0)
    const = lambda i: (0, 0)
    slab = lambda i: (layer, 0, 0)
    in_specs = [
        pl.BlockSpec((tm, D_MODEL), row),
        pl.BlockSpec((1, D_MODEL), const),
        pl.BlockSpec((None, D_MODEL, D_MODEL), slab),
        pl.BlockSpec((None, tm, PLE_DIM), lambda i: (layer, i, 0)),
        pl.BlockSpec((None, PLE_DIM, D_MODEL), slab),
    ]
    args = [h, g, w_pg, p, w_pp]
    if g_final is not None:
        in_specs.append(pl.BlockSpec((1, D_MODEL), const))
        args.append(g_final)
    return pl.pallas_call(
        functools.partial(_ple_kernel, final=g_final is not None),
        grid=(s // tm,),
        in_specs=in_specs,
        out_specs=pl.BlockSpec((tm, D_MODEL), row),
        out_shape=jax.ShapeDtypeStruct((s, D_MODEL), F32),
        compiler_params=_cparams(("parallel",)),
        name="ple",
    )(*args)


def kernel(x, p, g_mix, w_in, lb_logits, g_hg_norm, conv_w, w_gla_gate, b_gla_gate, g_gla_norm,
           w_out, g_mlp, w_up, w_down, g_ple, w_pg, w_pp, g_final):
    batch, seq, _ = x.shape
    depth = w_in.shape[0]
    assert batch == 1 and seq % 1024 == 0
    lb_cum = jnp.cumsum(jax.nn.softmax(lb_logits.astype(F32), axis=0), axis=0)
    lb_all = lb_cum - lb_cum[0:1]
    row = lambda a: a.reshape(1, -1).astype(F32)

    w_in_b = w_in.astype(BF16)
    w_ga = jnp.pad(w_in_b[:, :, IN_MAIN:], ((0, 0), (0, 0), (0, LANES - GLA_GATE_RANK)))
    w_gate = jnp.pad(w_gla_gate, ((0, 0), (0, LANES - GLA_GATE_RANK), (0, 0))).astype(BF16)
    w_out_b, w_up_b, w_down_b = w_out.astype(BF16), w_up.astype(BF16), w_down.astype(BF16)
    w_pg_b, w_pp_b = w_pg.astype(BF16), w_pp.astype(BF16)
    p2 = p.reshape(depth, seq, PLE_DIM)

    h = x.reshape(seq, D_MODEL)
    for l in range(depth):
        z, ga = _inproj(h, row(g_mix[l]), w_in_b, l, w_ga[l])
        y = _mixer(z, ga, row(lb_all[l]), row(g_hg_norm[l]), conv_w[l].astype(F32), w_gate[l],
                   row(b_gla_gate[l]), row(g_gla_norm[l]))
        h = _outproj(y, z, row(g_hg_norm[l]), row(g_gla_norm[l]), w_out_b, l, h)
        h = _mlp(h, row(g_mlp[l]), w_up_b, w_down_b, l)
        h = _ple(h, row(g_ple[l]), w_pg_b, p2, w_pp_b, l, row(g_final) if l == depth - 1 else None)
    return h.reshape(batch, seq, D_MODEL)
```

```python
import functools

import jax
import jax.numpy as jnp
from jax import lax
from jax.experimental import pallas as pl
from jax.experimental.pallas import tpu as pltpu

F32 = jnp.float32
BF16 = jnp.bfloat16

EPS = 1e-6
LOG2E = 1.4426950408889634
D_MODEL = 2048
D_FF = 4 * D_MODEL
PLE_DIM = 256
HG_WIDTH = 1024
HG_HEADS = 8
HEAD_DK = 128
HG_DV = 128
GLA_HEADS = 4
GLA_DV = 256
GLA_KEY_WIDTH = GLA_HEADS * HEAD_DK
GLA_WIDTH = GLA_HEADS * GLA_DV
GLA_GATE_RANK = 16
GLA_GATE_NORM = 16.0
CONV_WIDTH = 4
CONV_CH = 2 * GLA_KEY_WIDTH + GLA_WIDTH
IN_MAIN = 4 * HG_WIDTH + CONV_CH + GLA_WIDTH
LANES = 128
SUBLANES = 8

OFF_HQ, OFF_HF, OFF_HI, OFF_HG = 0, HG_WIDTH, 2 * HG_WIDTH, 3 * HG_WIDTH
OFF_CONV = 4 * HG_WIDTH
OFF_GR = OFF_CONV + CONV_CH
DECAY_W = HG_WIDTH + GLA_KEY_WIDTH

CHUNK = 64
SUB = 8
LOG2_SUB = SUB.bit_length() - 1
MIX_TILE = 256
HALO = SUBLANES

VMEM_LIMIT = 56 * 1024 * 1024


def _cparams(sem):
    return pltpu.CompilerParams(dimension_semantics=sem, vmem_limit_bytes=VMEM_LIMIT)


def _rms_scale(x, g):
    ms = jnp.mean(x * x, axis=-1, keepdims=True)
    return x * lax.rsqrt(ms + EPS) * g


def _dot(a, b):
    return jnp.dot(a, b, preferred_element_type=F32)


def _dot_nt(a, b):
    return lax.dot_general(a, b, (((1,), (1,)), ((), ())), preferred_element_type=F32)


def _dot_tn(a, b):
    return lax.dot_general(a, b, (((0,), (0,)), ((), ())), preferred_element_type=F32)


def _inproj_kernel(h_ref, g_ref, w_ref, wga_ref, z_ref, ga_ref, u_ref):
    @pl.when(pl.program_id(1) == 0)
    def _():
        u_ref[...] = _rms_scale(h_ref[...], g_ref[...]).astype(BF16)
        ga_ref[...] = _dot(u_ref[...], wga_ref[...])

    z_ref[...] = _dot(u_ref[...], w_ref[...]).astype(z_ref.dtype)


def _inproj(h, g, w_in, layer, w_ga, tm=1024, tn=1792):
    s = h.shape[0]
    return pl.pallas_call(
        _inproj_kernel,
        grid=(s // tm, IN_MAIN // tn),
        in_specs=[
            pl.BlockSpec((tm, D_MODEL), lambda i, j: (i, 0)),
            pl.BlockSpec((1, D_MODEL), lambda i, j: (0, 0)),
            pl.BlockSpec((None, D_MODEL, tn), lambda i, j: (layer, 0, j)),
            pl.BlockSpec((D_MODEL, LANES), lambda i, j: (0, 0)),
        ],
        out_specs=[
            pl.BlockSpec((tm, tn), lambda i, j: (i, j)),
            pl.BlockSpec((tm, LANES), lambda i, j: (i, 0)),
        ],
        out_shape=[
            jax.ShapeDtypeStruct((s, IN_MAIN), BF16),
            jax.ShapeDtypeStruct((s, LANES), F32),
        ],
        scratch_shapes=[pltpu.VMEM((tm, D_MODEL), BF16)],
        compiler_params=_cparams(("parallel", "arbitrary")),
        name="inproj",
    )(h, g, w_in, w_ga)


def _log2_1p_exp2_neg_abs(x):
    return jnp.log(1.0 + jnp.exp2(-jnp.abs(x))) * LOG2E


def _log2_sigmoid(x2):
    return jnp.minimum(x2, 0.0) - _log2_1p_exp2_neg_abs(x2)


def _pad_rows(x, start):
    parts = []
    if start:
        parts.append(jnp.zeros((start, x.shape[1]), x.dtype))
    parts.append(x)
    if start + x.shape[0] < CHUNK:
        parts.append(jnp.zeros((CHUNK - start - x.shape[0], x.shape[1]), x.dtype))
    return jnp.concatenate(parts, axis=0) if len(parts) > 1 else x


def _keys_log(keys):
    return keys[0] == "log"


def _diag_terms(q, b_ref, bcol, keys):
    kcol = keys[-1]
    bv = b_ref[pl.ds(HALO, CHUNK), bcol]
    slabs = []
    for d in range(CHUNK // SUB):
        qb = q[d * SUB:(d + 1) * SUB]
        bb = bv[d * SUB:(d + 1) * SUB]
        xs = []
        for j in range(d * SUB, (d + 1) * SUB):
            if _keys_log(keys):
                e = jnp.exp2(jnp.minimum(bb - keys[1][pl.ds(j, 1), kcol],
                                         keys[2][pl.ds(j, 1), kcol]))
            else:
                e = (jnp.exp2(jnp.minimum(bb - b_ref[pl.ds(HALO + j, 1), bcol], 0.0))
                     * keys[1][pl.ds(j, 1), kcol])
            xs.append(qb * e)
        slabs.append(jnp.concatenate(xs, axis=1))
    return jnp.concatenate(slabs, axis=0).astype(BF16)


def _head_chunk(q, v_bf, b_ref, bcol, keys, st_ref, stb_ref, hidx, m_pair):
    log_keys = _keys_log(keys)
    kcol = keys[-1]

    def brow(j):
        return b_ref[pl.ds(HALO + j, 1), bcol]

    bv = b_ref[pl.ds(HALO, CHUNK), bcol]
    b_prev = brow(-1)
    b_last = brow(CHUNK - 1)
    if log_keys:
        lkv = keys[2][:, kcol]
    else:
        kv = keys[1][:, kcol]

    def q_side(r, lo=0, hi=CHUNK):
        return q[lo:hi] * jnp.exp2(bv[lo:hi] - r)

    def k_side(r, lo=0, hi=CHUNK):
        if log_keys:
            return jnp.exp2(r - bv[lo:hi] + lkv[lo:hi])
        return kv[lo:hi] * jnp.exp2(r - bv[lo:hi])

    o = _dot_nt(q_side(b_prev).astype(BF16), stb_ref[hidx])
    st = st_ref[hidx] * jnp.exp2(b_last - b_prev) + _dot_tn(v_bf, k_side(b_last).astype(BF16))
    st_ref[hidx] = st
    stb_ref[hidx] = st.astype(BF16)

    q_slabs, k_slabs = [], []
    n = 2 * SUB
    while n < CHUNK:
        for p in range(CHUNK // (2 * n)):
            left, right = 2 * n * p, 2 * n * p + n
            ref = brow(right - 1)
            q_slabs.append(_pad_rows(q_side(ref, right, right + n), right))
            k_slabs.append(_pad_rows(k_side(ref, left, right), left))
        n *= 2
    a = _dot_nt(jnp.concatenate(q_slabs, axis=1).astype(BF16),
                jnp.concatenate(k_slabs, axis=1).astype(BF16))
    q1, k1 = [], []
    for p in range(CHUNK // (2 * SUB)):
        left, right = 2 * SUB * p, 2 * SUB * p + SUB
        ref = brow(right - 1)
        zeros = jnp.zeros((SUB, HEAD_DK), F32)
        q1 += [zeros, q_side(ref, right, right + SUB)]
        k1 += [k_side(ref, left, right), zeros]
    a1 = _dot_nt(jnp.concatenate(q1, axis=0).astype(BF16), jnp.concatenate(k1, axis=0).astype(BF16))
    a = jnp.where(m_pair, a1, a)
    return o, a


def _head_norm_gate(o, gain, gate):
    ms = jnp.mean(o * o, axis=-1, keepdims=True)
    return o * lax.rsqrt(ms + EPS) * gain * (gate * jax.nn.sigmoid(gate))


def _silu(x):
    return x * jax.nn.sigmoid(x)


def _mixer_kernel(z_ref, ga_ref, lb_ref, ghg_ref, convw_ref, wgate_ref, bgate_ref, ggla_ref, rowops_ref,
                  emat_ref, y_ref, sth_ref, stg_ref, sthb_ref, stgb_ref, xh_ref, c_ref, lk_ref, gq_ref, gk_ref,
                  gv_ref, b_ref, o_ref, xd_ref, sd_ref):
    tile = y_ref.shape[0]

    @pl.when(pl.program_id(0) == 0)
    def _():
        sth_ref[...] = jnp.zeros_like(sth_ref)
        stg_ref[...] = jnp.zeros_like(stg_ref)
        sthb_ref[...] = jnp.zeros_like(sthb_ref)
        stgb_ref[...] = jnp.zeros_like(stgb_ref)
        xh_ref[pl.ds(0, HALO), :] = jnp.zeros((HALO, CONV_CH), F32)
        b_ref[pl.ds(0, HALO), :] = jnp.zeros((HALO, DECAY_W), F32)

    lb = lb_ref[...]
    l2_lb = jnp.log(lb) * LOG2E
    l2_1m = jnp.log1p(-lb) * LOG2E
    h2 = z_ref[:, OFF_HF:OFF_HF + HG_WIDTH].astype(F32) * LOG2E
    rhs = l2_1m + _log2_sigmoid(h2)
    log2_f = jnp.maximum(l2_lb, rhs) + _log2_1p_exp2_neg_abs(l2_lb - rhs)
    lk = rhs - h2
    lk_ref[...] = lk

    g2 = (_dot(ga_ref[...].astype(BF16), wgate_ref[...]) + bgate_ref[...]) * LOG2E
    log2_alpha = _log2_sigmoid(g2) * (1.0 / GLA_GATE_NORM)

    tri = rowops_ref[0]
    logd = jnp.concatenate([log2_f, log2_alpha], axis=1)
    hi = logd.astype(BF16)
    lo = (logd - hi.astype(F32)).astype(BF16)
    bcum = _dot(tri, hi) + _dot(tri, lo)
    b_ref[pl.ds(HALO, tile), :] = bcum
    c_ref[...] = bcum[:, :HG_WIDTH] - lk

    x_bf = z_ref[:, OFF_CONV:OFF_CONV + CONV_CH]
    conv = convw_ref[CONV_WIDTH - 1:CONV_WIDTH, :] * x_bf.astype(F32)
    for j in range(CONV_WIDTH - 1):
        conv = conv + convw_ref[j:j + 1, :] * _dot(rowops_ref[CONV_WIDTH - 1 - j], x_bf)
    xh_ref[pl.ds(HALO, HALO), :] = x_bf[:HALO].astype(F32)
    head = convw_ref[0:1, :] * xh_ref[pl.ds(HALO - CONV_WIDTH + 1, HALO), :]
    for j in range(1, CONV_WIDTH):
        head = head + convw_ref[j:j + 1, :] * xh_ref[pl.ds(HALO - CONV_WIDTH + 1 + j, HALO), :]
    xh_ref[pl.ds(0, HALO), :] = x_bf[tile - HALO:].astype(F32)

    def put_qkv(rows, act):
        gq_ref[rows, :] = act[:, :GLA_KEY_WIDTH] * (HEAD_DK ** -0.5)
        gk_ref[rows, :] = act[:, GLA_KEY_WIDTH:2 * GLA_KEY_WIDTH]
        gv_ref[rows, :] = act[:, 2 * GLA_KEY_WIDTH:].astype(BF16)

    put_qkv(pl.ds(0, tile), _silu(conv))
    put_qkv(pl.ds(0, HALO), _silu(head))

    e_mat = emat_ref[...]
    tt = lax.broadcasted_iota(jnp.int32, (CHUNK, CHUNK), 0)
    ss = lax.broadcasted_iota(jnp.int32, (CHUNK, CHUNK), 1)
    m_diag = (jnp.right_shift(tt, LOG2_SUB) == jnp.right_shift(ss, LOG2_SUB)) & (tt >= ss)
    m_pair = (jnp.right_shift(tt, LOG2_SUB + 1) == jnp.right_shift(ss, LOG2_SUB + 1)) & (tt >= ss)

    n_heads = HG_HEADS + GLA_HEADS
    group = 4
    lag = 4

    def chunk_heads(r0):
        rows = pl.ds(r0, CHUNK)
        b_view = b_ref.at[pl.ds(r0, HALO + CHUNK)]
        c_view, lk_view, gk_view = c_ref.at[rows], lk_ref.at[rows], gk_ref.at[rows]
        heads = []
        for h in range(HG_HEADS):
            cs = slice(h * HEAD_DK, (h + 1) * HEAD_DK)
            q = z_ref[rows, OFF_HQ + h * HEAD_DK:OFF_HQ + (h + 1) * HEAD_DK].astype(F32)
            v_bf = z_ref[rows, OFF_HI + h * HG_DV:OFF_HI + (h + 1) * HG_DV]
            heads.append((q, v_bf, cs, ("log", c_view, lk_view, cs), sth_ref, sthb_ref, h,
                          slice(h * HG_DV, (h + 1) * HG_DV)))
        for h in range(GLA_HEADS):
            ks = slice(h * HEAD_DK, (h + 1) * HEAD_DK)
            vs = slice(h * GLA_DV, (h + 1) * GLA_DV)
            bs = slice(HG_WIDTH + h * HEAD_DK, HG_WIDTH + (h + 1) * HEAD_DK)
            heads.append((gq_ref[rows, ks], gv_ref[rows, vs], bs, ("lin", gk_view, ks), stg_ref, stgb_ref,
                          h, slice(HG_WIDTH + h * GLA_DV, HG_WIDTH + (h + 1) * GLA_DV)))
        return rows, b_view, heads

    def diag_group(chunk, g, slot):
        _, b_view, heads = chunk
        for i in range(g * group, (g + 1) * group):
            q, _, bcol, keys = heads[i][:4]
            xd_ref[pl.ds(i * CHUNK, CHUNK), :] = _diag_terms(q, b_view, bcol, keys)
        grows = pl.ds(g * group * CHUNK, group * CHUNK)
        sd_ref[slot, grows, :] = _dot(xd_ref[grows, :], e_mat)

    def rest_group(chunk, g, slot, pending):
        rows, b_view, heads = chunk
        for i in range(g * group, (g + 1) * group):
            q, v_bf, bcol, keys, st_ref, stb_ref, hidx, ocol = heads[i]
            o, a = _head_chunk(q, v_bf, b_view, bcol, keys, st_ref, stb_ref, hidx, m_pair)
            pending = pending + [(rows, slot, i, o, a, v_bf, ocol)]
            if len(pending) > lag:
                finish(*pending[0])
                pending = pending[1:]
        return pending

    def finish(rows, slot, i, o, a, v_bf, ocol):
        sd = sd_ref[slot, pl.ds(i * CHUNK, CHUNK), :]
        a = jnp.where(m_diag, sd[:, :CHUNK], a)
        o_ref[rows, ocol] = o + _dot(a.astype(BF16), v_bf)

    n_chunks = tile // CHUNK
    first = chunk_heads(0)
    for g in range(n_heads // group):
        diag_group(first, g, 0)

    def chunk_body(c, carry):
        cur = chunk_heads(pl.multiple_of(c * CHUNK, CHUNK))
        nxt = chunk_heads(pl.multiple_of((c + 1) * CHUNK, CHUNK))
        slot = c & 1
        pending = []
        for g in range(n_heads // group):
            diag_group(nxt, g, 1 - slot)
            pending = rest_group(cur, g, slot, pending)
        for item in pending:
            finish(*item)
        return carry

    lax.fori_loop(0, n_chunks - 1, chunk_body, 0)
    last = chunk_heads((n_chunks - 1) * CHUNK)
    pending = []
    for g in range(n_heads // group):
        pending = rest_group(last, g, (n_chunks - 1) & 1, pending)
    for item in pending:
        finish(*item)

    for h in range(HG_HEADS):
        cs = slice(h * HG_DV, (h + 1) * HG_DV)
        gate = z_ref[:, OFF_HG + h * HG_DV:OFF_HG + (h + 1) * HG_DV].astype(F32)
        y_ref[:, cs] = _head_norm_gate(o_ref[:, cs], ghg_ref[:, cs], gate).astype(y_ref.dtype)
    for h in range(GLA_HEADS):
        vs = slice(h * GLA_DV, (h + 1) * GLA_DV)
        gate = z_ref[:, OFF_GR + h * GLA_DV:OFF_GR + (h + 1) * GLA_DV].astype(F32)
        y_ref[:, HG_WIDTH + h * GLA_DV:HG_WIDTH + (h + 1) * GLA_DV] = _head_norm_gate(
            o_ref[:, HG_WIDTH + h * GLA_DV:HG_WIDTH + (h + 1) * GLA_DV], ggla_ref[:, vs],
            gate).astype(y_ref.dtype)


def _mixer(z, ga, lb, g_hg, conv_w, w_gate, b_gate, g_gla, tile=MIX_TILE):
    s = z.shape[0]
    const = lambda i: (0, 0)
    lag = jnp.arange(tile)[:, None] - jnp.arange(tile)[None, :]
    rowops = jnp.stack([lag >= 0] + [lag == d for d in range(1, CONV_WIDTH)]).astype(BF16)
    e_mat = ((jnp.arange(SUB * HEAD_DK)[:, None] // HEAD_DK) == (jnp.arange(LANES)[None, :] % SUB)).astype(BF16)
    return pl.pallas_call(
        _mixer_kernel,
        grid=(s // tile,),
        in_specs=[
            pl.BlockSpec((tile, IN_MAIN), lambda i: (i, 0)),
            pl.BlockSpec((tile, LANES), lambda i: (i, 0)),
            pl.BlockSpec((1, HG_WIDTH), const),
            pl.BlockSpec((1, HG_WIDTH), const),
            pl.BlockSpec((CONV_WIDTH, CONV_CH), const),
            pl.BlockSpec((LANES, GLA_KEY_WIDTH), const),
            pl.BlockSpec((1, GLA_KEY_WIDTH), const),
            pl.BlockSpec((1, GLA_WIDTH), const),
            pl.BlockSpec((CONV_WIDTH, tile, tile), lambda i: (0, 0, 0)),
            pl.BlockSpec((SUB * HEAD_DK, LANES), const),
        ],
        out_specs=pl.BlockSpec((tile, D_MODEL), lambda i: (i, 0)),
        out_shape=jax.ShapeDtypeStruct((s, D_MODEL), BF16),
        scratch_shapes=[
            pltpu.VMEM((HG_HEADS, HG_DV, HEAD_DK), F32),
            pltpu.VMEM((GLA_HEADS, GLA_DV, HEAD_DK), F32),
            pltpu.VMEM((HG_HEADS, HG_DV, HEAD_DK), BF16),
            pltpu.VMEM((GLA_HEADS, GLA_DV, HEAD_DK), BF16),
            pltpu.VMEM((2 * HALO, CONV_CH), F32),
            pltpu.VMEM((tile, HG_WIDTH), F32),
            pltpu.VMEM((tile, HG_WIDTH), F32),
            pltpu.VMEM((tile, GLA_KEY_WIDTH), F32),
            pltpu.VMEM((tile, GLA_KEY_WIDTH), F32),
            pltpu.VMEM((tile, GLA_WIDTH), BF16),
            pltpu.VMEM((HALO + tile, DECAY_W), F32),
            pltpu.VMEM((tile, D_MODEL), F32),
            pltpu.VMEM(((HG_HEADS + GLA_HEADS) * CHUNK, SUB * HEAD_DK), BF16),
            pltpu.VMEM((2, (HG_HEADS + GLA_HEADS) * CHUNK, LANES), F32),
        ],
        compiler_params=_cparams(("arbitrary",)),
        name="mixer",
    )(z, ga, lb, g_hg, conv_w, w_gate, b_gate, g_gla, rowops, e_mat)


def _outproj_kernel(y_ref, w_ref, h_ref, o_ref):
    o_ref[...] = h_ref[...] + _dot(y_ref[...], w_ref[...])


def _outproj(y, w, layer, h, tm=512):
    s = h.shape[0]
    row = lambda i: (i, 0)
    return pl.pallas_call(
        _outproj_kernel,
        grid=(s // tm,),
        in_specs=[
            pl.BlockSpec((tm, D_MODEL), row),
            pl.BlockSpec((None, D_MODEL, D_MODEL), lambda i: (layer, 0, 0)),
            pl.BlockSpec((tm, D_MODEL), row),
        ],
        out_specs=pl.BlockSpec((tm, D_MODEL), row),
        out_shape=jax.ShapeDtypeStruct((s, D_MODEL), F32),
        compiler_params=_cparams(("parallel",)),
        name="outproj",
    )(y, w, h)


def _mlp_kernel(h_ref, g_ref, wup_ref, wdown_ref, o_ref, u_ref):
    @pl.when(pl.program_id(1) == 0)
    def _():
        x = h_ref[...]
        u_ref[...] = _rms_scale(x, g_ref[...]).astype(BF16)
        o_ref[...] = x

    m = jnp.maximum(_dot(u_ref[...], wup_ref[...]), 0.0)
    o_ref[...] += _dot((m * m).astype(BF16), wdown_ref[...])


def _mlp(h, g, w_up, w_down, layer, tm=512, tf=1024):
    s = h.shape[0]
    return pl.pallas_call(
        _mlp_kernel,
        grid=(s // tm, D_FF // tf),
        in_specs=[
            pl.BlockSpec((tm, D_MODEL), lambda i, f: (i, 0)),
            pl.BlockSpec((1, D_MODEL), lambda i, f: (0, 0)),
            pl.BlockSpec((None, D_MODEL, tf), lambda i, f: (layer, 0, f)),
            pl.BlockSpec((None, tf, D_MODEL), lambda i, f: (layer, f, 0)),
        ],
        out_specs=pl.BlockSpec((tm, D_MODEL), lambda i, f: (i, 0)),
        out_shape=jax.ShapeDtypeStruct((s, D_MODEL), F32),
        scratch_shapes=[pltpu.VMEM((tm, D_MODEL), BF16)],
        compiler_params=_cparams(("parallel", "arbitrary")),
        name="mlp",
    )(h, g, w_up, w_down)


def _ple_kernel(h_ref, g_ref, wpg_ref, p_ref, wpp_ref, *rest, final):
    x = h_ref[...]
    u = _rms_scale(x, g_ref[...]).astype(BF16)
    gate = jax.nn.sigmoid(_dot(u, wpg_ref[...]))
    y = x + gate * _dot(p_ref[...].astype(BF16), wpp_ref[...])
    if final:
        gf_ref, o_ref = rest
        y = _rms_scale(y, gf_ref[...])
    else:
        (o_ref,) = rest
    o_ref[...] = y


def _ple(h, g, w_pg, p, w_pp, layer, g_final=None, tm=512):
    s = h.shape[0]
    row = lambda i: (i, 0)
    const = lambda i: (0, 0)
    slab = lambda i: (layer, 0, 0)
    in_specs = [
        pl.BlockSpec((tm, D_MODEL), row),
        pl.BlockSpec((1, D_MODEL), const),
        pl.BlockSpec((None, D_MODEL, D_MODEL), slab),
        pl.BlockSpec((None, tm, PLE_DIM), lambda i: (layer, i, 0)),
        pl.BlockSpec((None, PLE_DIM, D_MODEL), slab),
    ]
    args = [h, g, w_pg, p, w_pp]
    if g_final is not None:
        in_specs.append(pl.BlockSpec((1, D_MODEL), const))
        args.append(g_final)
    return pl.pallas_call(
        functools.partial(_ple_kernel, final=g_final is not None),
        grid=(s // tm,),
        in_specs=in_specs,
        out_specs=pl.BlockSpec((tm, D_MODEL), row),
        out_shape=jax.ShapeDtypeStruct((s, D_MODEL), F32),
        compiler_params=_cparams(("parallel",)),
        name="ple",
    )(*args)


def kernel(x, p, g_mix, w_in, lb_logits, g_hg_norm, conv_w, w_gla_gate, b_gla_gate, g_gla_norm,
           w_out, g_mlp, w_up, w_down, g_ple, w_pg, w_pp, g_final):
    batch, seq, _ = x.shape
    depth = w_in.shape[0]
    assert batch == 1 and seq % 1024 == 0
    lb_cum = jnp.cumsum(jax.nn.softmax(lb_logits.astype(F32), axis=0), axis=0)
    lb_all = lb_cum - lb_cum[0:1]
    row = lambda a: a.reshape(1, -1).astype(F32)

    w_in_b = w_in.astype(BF16)
    w_ga = jnp.pad(w_in_b[:, :, IN_MAIN:], ((0, 0), (0, 0), (0, LANES - GLA_GATE_RANK)))
    w_gate = jnp.pad(w_gla_gate, ((0, 0), (0, LANES - GLA_GATE_RANK), (0, 0))).astype(BF16)
    w_out_b, w_up_b, w_down_b = w_out.astype(BF16), w_up.astype(BF16), w_down.astype(BF16)
    w_pg_b, w_pp_b = w_pg.astype(BF16), w_pp.astype(BF16)
    p2 = p.reshape(depth, seq, PLE_DIM)

    h = x.reshape(seq, D_MODEL)
    for l in range(depth):
        z, ga = _inproj(h, row(g_mix[l]), w_in_b, l, w_ga[l])
        y = _mixer(z, ga, row(lb_all[l]), row(g_hg_norm[l]), conv_w[l].astype(F32), w_gate[l],
                   row(b_gla_gate[l]), row(g_gla_norm[l]))
        h = _outproj(y, w_out_b, l, h)
        h = _mlp(h, row(g_mlp[l]), w_up_b, w_down_b, l)
        h = _ple(h, row(g_ple[l]), w_pg_b, p2, w_pp_b, l, row(g_final) if l == depth - 1 else None)
    return h.reshape(batch, seq, D_MODEL)
```
